```python
import math
import jax
import jax.numpy as jnp
from jax import lax
import numpy as np

D_MODEL = 1024
BATCH = 8
SEQ = 2048
DEPTH = 4
DEC_BATCH = 128
DEC_SEQ = 1
PAST_LEN = 8192
PAGE_SIZE = 128

HEAD_DIM = 64
A_WINDOWS = (128, 512, 2048)
A_DILATIONS = (1, 4, 16)
A_GROUPS = len(A_WINDOWS)
A_HEADS = 4
A_WIDTH = A_HEADS * HEAD_DIM
A_QKV = A_GROUPS * A_HEADS * HEAD_DIM
B_WINDOW = 128
B_Q_HEADS = 16
B_KV_HEADS = 2
B_REP = B_Q_HEADS // B_KV_HEADS
B_WIDTH = B_Q_HEADS * HEAD_DIM
C_HEADS = 4
C_HEAD_DIM = 256
C_WIDTH = C_HEADS * C_HEAD_DIM
C_CONV = 4
C_CHUNK = 128
N_BUCKETS = 32
MAX_DISTANCE = 2048
A_BIAS_OFF = 0
B_BIAS_OFF = A_GROUPS * A_HEADS
N_BIAS_HEADS = B_BIAS_OFF + B_Q_HEADS
LOG_BUCKET_RATIO = math.log(MAX_DISTANCE / (N_BUCKETS // 2))
BAND = 128
N_EXPERTS = 32
TOP_K = 4
D_FF = 1024
SWIGLU_LIMIT = 7.0
SWIGLU_ALPHA = 1.702
MOE_BLOCK = 128
DN_ALPHA = (2 * DEPTH) ** 0.25
DN_BETA = (8 * DEPTH) ** -0.25
LN_EPS = 1e-5
NEG = -1e30
SPLITS = (A_QKV, A_QKV, A_QKV,
          B_WIDTH, B_KV_HEADS * HEAD_DIM, B_KV_HEADS * HEAD_DIM,
          C_WIDTH, C_WIDTH, C_WIDTH, C_HEADS, C_HEADS, C_WIDTH)
SPLIT_POINTS = tuple(int(p) for p in np.cumsum(SPLITS)[:-1])
D_IN = sum(SPLITS)

kernel_name = 'hybrid_dilated_swa_mlstm_moe_step'


def layer_norm(x, g, b):
    xf = x.astype(jnp.float32)
    mu = xf.mean(-1, keepdims=True)
    var = jnp.square(xf - mu).mean(-1, keepdims=True)
    return ((xf - mu) * lax.rsqrt(var + LN_EPS) * g + b).astype(x.dtype)


def rel_bucket(dist):
    exact = N_BUCKETS // 2
    d = jnp.maximum(dist, 1).astype(jnp.float32)
    far = exact + (jnp.log(d / exact) / LOG_BUCKET_RATIO * (N_BUCKETS - exact)).astype(jnp.int32)
    return jnp.where(dist < exact, dist, jnp.minimum(far, N_BUCKETS - 1))


def band_attention(q, k, v, dilation, n_back, bias_table, sink):
    N, M, G, R, dh = q.shape
    nb = -(-M // BAND)
    mp = nb * BAND
    qb = jnp.pad(q, ((0, 0), (0, mp - M), (0, 0), (0, 0), (0, 0))).reshape(N, nb, BAND, G, R, dh)
    kv_pad = ((0, 0), (BAND, mp - M), (0, 0), (0, 0))
    kp = jnp.pad(k, kv_pad).reshape(N, nb + 1, BAND, G, dh)
    vp = jnp.pad(v, kv_pad).reshape(N, nb + 1, BAND, G, dh)
    kb = jnp.concatenate([kp[:, :-1], kp[:, 1:]], axis=2)
    vb = jnp.concatenate([vp[:, :-1], vp[:, 1:]], axis=2)
    logits = jnp.einsum('nbqgrd,nbkgd->nbgrqk', qb, kb).astype(jnp.float32) * dh ** -0.5
    dist = BAND + jnp.arange(BAND)[:, None] - jnp.arange(2 * BAND)[None, :]
    bias = bias_table[rel_bucket(jnp.maximum(dist, 0) * dilation)]
    bias = bias.transpose(2, 0, 1).reshape(G, R, BAND, 2 * BAND).astype(jnp.float32)
    key_pos = jnp.arange(nb)[:, None] * BAND + jnp.arange(2 * BAND)[None, :] - BAND
    mask = ((dist >= 0) & (dist <= n_back))[None] & (key_pos >= 0)[:, None, :]
    logits = jnp.where(mask[None, :, None, None], logits + bias, NEG)
    lse = jax.nn.logsumexp(logits, axis=-1)
    denom = lse if sink is None else jnp.logaddexp(lse, sink[:, :, None])
    p = jnp.exp(logits - denom[..., None]).astype(v.dtype)
    out = jnp.einsum('nbgrqk,nbkgd->nbqgrd', p, vb).reshape(N, mp, G, R, dh)[:, :M]
    lse = jnp.moveaxis(lse, -1, 2).reshape(N, mp, G, R)[:, :M]
    return out, lse


def gathered_attention(q, kv, dilation, n_back, bias_table, sink):
    N, T, G, R, dh = q.shape
    L = kv.shape[1] - T
    j = jnp.arange(n_back + 1)
    pos = PAST_LEN + jnp.arange(T)[:, None] - j[None, :] * dilation
    idx = pos - (PAST_LEN - L)
    valid = (pos >= 0) & (idx >= 0)
    kg = kv[:, jnp.clip(idx, 0, L + T - 1)]
    logits = jnp.einsum('ntgrd,ntjgd->ntgrj', q, kg[:, :, :, 0]).astype(jnp.float32) * dh ** -0.5
    bias = bias_table[rel_bucket(j * dilation)].T.reshape(G, R, n_back + 1).astype(jnp.float32)
    logits = jnp.where(valid[None, :, None, None, :], logits + bias, NEG)
    lse = jax.nn.logsumexp(logits, axis=-1)
    denom = lse if sink is None else jnp.logaddexp(lse, sink)
    p = jnp.exp(logits - denom[..., None]).astype(kv.dtype)
    out = jnp.einsum('ntgrj,ntjgd->ntgrd', p, kg[:, :, :, 1])
    return out, lse


def combine_dilations(outs, lses):
    w = jax.nn.softmax(jnp.stack(lses, 0).astype(jnp.float32), axis=0)
    return jnp.sum(w[..., None] * jnp.stack(outs, 0).astype(jnp.float32), axis=0)


def causal_conv(x, buf, w, b):
    T = x.shape[1]
    xp = jnp.concatenate([buf, x], axis=1)
    y = sum(xp[:, j:j + T] * w[j] for j in range(C_CONV)) + b
    return y, xp[:, T:]


def mlstm_chunkwise(q, k, v, log_f, i_pre, mem0, norm0, max0, chunk):
    B, H, S, dk = q.shape
    dv = v.shape[-1]
    nc = S // chunk

    def chunks(t):
        return jnp.moveaxis(t.reshape(B, H, nc, chunk, *t.shape[3:]), 2, 0)

    causal = jnp.tril(jnp.ones((chunk, chunk), bool))

    def step(carry, inp):
        mem, nrm, m = carry
        qc, kc, vc, lf, ig = inp
        a = jnp.cumsum(lf, axis=-1)
        inter = a + m[..., None]
        dmat = jnp.where(causal, a[..., :, None] - a[..., None, :] + ig[..., None, :], NEG)
        mt = jnp.maximum(inter, dmat.max(-1))
        w_inter = jnp.exp(inter - mt)
        s = jnp.einsum('bhtd,bhsd->bhts', qc, kc) * jnp.exp(dmat - mt[..., None])
        num = w_inter[..., None] * jnp.einsum('bhtd,bhde->bhte', qc, mem) + jnp.einsum('bhts,bhse->bhte', s, vc)
        den = w_inter * jnp.einsum('bhtd,bhd->bht', qc, nrm) + s.sum(-1)
        h = num / jnp.maximum(jnp.abs(den), jnp.exp(-mt))[..., None]
        m_new = mt[..., -1]
        a_last = a[..., -1:]
        decay = jnp.exp(a_last[..., 0] + m - m_new)
        w_s = jnp.exp(a_last - a + ig - m_new[..., None])
        mem_new = decay[..., None, None] * mem + jnp.einsum('bhs,bhsd,bhse->bhde', w_s, kc, vc)
        nrm_new = decay[..., None] * nrm + jnp.einsum('bhs,bhsd->bhd', w_s, kc)
        return (mem_new, nrm_new, m_new), h

    (mem1, norm1, max1), h = lax.scan(step, (mem0, norm0, max0),
                                      (chunks(q), chunks(k), chunks(v), chunks(log_f), chunks(i_pre)))
    return jnp.moveaxis(h, 0, 2).reshape(B, H, S, dv), (mem1, norm1, max1)


def token_mixer(u, rel_bias, in_w, conv_w, conv_b, igate_b, fgate_b, sink, br_a_w, br_b_w, br_c_w,
                gate_w, gate_b, out_w, cache):
    B, S, _ = u.shape
    f32 = jnp.float32
    aq, ak, av, bq, bk, bv, cq, ck, cv, ci, cf, co = jnp.split(u @ in_w, SPLIT_POINTS, axis=-1)
    aq = aq.reshape(B, S, A_GROUPS, A_HEADS, HEAD_DIM)
    a_kv = jnp.stack([ak, av], axis=2).reshape(B, S, 2, A_GROUPS, A_HEADS, HEAD_DIM)
    bq = bq.reshape(B, S, B_KV_HEADS, B_REP, HEAD_DIM)
    b_kv = jnp.stack([bk, bv], axis=2).reshape(B, S, 2, B_KV_HEADS, HEAD_DIM)
    b_sink = sink.reshape(B_KV_HEADS, B_REP).astype(f32)
    b_bias = rel_bias[:, B_BIAS_OFF:B_BIAS_OFF + B_Q_HEADS]
    a_outs, a_lses, a_bufs = [], [], []
    if cache is None:
        for g in range(A_GROUPS):
            d = A_DILATIONS[g]
            M = S // d

            def to_sub(t):
                return t.reshape(B, M, d, *t.shape[2:]).swapaxes(1, 2).reshape(B * d, M, *t.shape[2:])

            o, l = band_attention(to_sub(aq[:, :, g])[:, :, :, None], to_sub(a_kv[:, :, 0, g]),
                                  to_sub(a_kv[:, :, 1, g]), d, A_WINDOWS[g] // d,
                                  rel_bias[:, A_BIAS_OFF + g * A_HEADS:A_BIAS_OFF + (g + 1) * A_HEADS], None)
            a_outs.append(o.reshape(B, d, M, A_HEADS, HEAD_DIM).swapaxes(1, 2).reshape(B, S, A_HEADS, HEAD_DIM))
            a_lses.append(l.reshape(B, d, M, A_HEADS).swapaxes(1, 2).reshape(B, S, A_HEADS))
            a_bufs.append(a_kv[:, S - min(A_WINDOWS[g], S):, :, g])
        ob, _ = band_attention(bq, b_kv[:, :, 0], b_kv[:, :, 1], 1, B_WINDOW, b_bias, b_sink)
        b_buf = b_kv[:, S - min(B_WINDOW, S):]
        mem0 = jnp.zeros((B, C_HEADS, C_HEAD_DIM, C_HEAD_DIM), f32)
        norm0 = jnp.zeros((B, C_HEADS, C_HEAD_DIM), f32)
        max0 = jnp.zeros((B, C_HEADS), f32)
        conv0 = jnp.zeros((B, C_CONV - 1, 2 * C_WIDTH), u.dtype)
        chunk = min(C_CHUNK, S)
    else:
        a1, a2, a3, b_cache, mem0, norm0, max0, conv0 = cache
        for g, a_cache in enumerate((a1, a2, a3)):
            d = A_DILATIONS[g]
            kv_all = jnp.concatenate([a_cache.astype(u.dtype), a_kv[:, :, :, g]], axis=1)
            o, l = gathered_attention(aq[:, :, g, :, None], kv_all, d, A_WINDOWS[g] // d,
                                      rel_bias[:, A_BIAS_OFF + g * A_HEADS:A_BIAS_OFF + (g + 1) * A_HEADS], None)
            a_outs.append(o[:, :, :, 0])
            a_lses.append(l[:, :, :, 0])
            a_bufs.append(kv_all[:, S:])
        kv_all = jnp.concatenate([b_cache.astype(u.dtype), b_kv], axis=1)
        ob, _ = gathered_attention(bq, kv_all, 1, B_WINDOW, b_bias, b_sink)
        b_buf = kv_all[:, S:]
        chunk = S
    qk, conv_new = causal_conv(jnp.concatenate([cq, ck], axis=-1), conv0.astype(u.dtype), conv_w, conv_b)
    qk = jax.nn.silu(qk)

    def heads(t):
        return t.reshape(B, S, C_HEADS, C_HEAD_DIM).transpose(0, 2, 1, 3).astype(f32)

    log_f = jax.nn.log_sigmoid((cf + fgate_b).astype(f32)).transpose(0, 2, 1)
    i_pre = (ci + igate_b).astype(f32).transpose(0, 2, 1)
    h, (mem1, norm1, max1) = mlstm_chunkwise(heads(qk[..., :C_WIDTH]), heads(qk[..., C_WIDTH:]) * C_HEAD_DIM ** -0.5,
                                             heads(cv), log_f, i_pre, mem0.astype(f32), norm0.astype(f32),
                                             max0.astype(f32), chunk)
    oc = h.transpose(0, 2, 1, 3).reshape(B, S, C_WIDTH) * jax.nn.sigmoid(co.astype(f32))
    oa = combine_dilations(a_outs, a_lses).reshape(B, S, A_WIDTH)
    gates = jax.nn.sigmoid((u @ gate_w + gate_b).astype(f32)).astype(u.dtype)
    ga, gb, gc = jnp.split(gates, 3, axis=-1)
    merged = (ga * (oa.astype(u.dtype) @ br_a_w) + gb * (ob.reshape(B, S, B_WIDTH) @ br_b_w)
              + gc * (oc.astype(u.dtype) @ br_c_w))
    return merged @ out_w, (a_bufs[0], a_bufs[1], a_bufs[2], b_buf, mem1, norm1, max1, conv_new)


def moe_ffn(u, router_w, router_b, up_w, up_b, down_w, down_b):
    shp = u.shape
    x = u.reshape(-1, shp[-1])
    T = x.shape[0]
    logits = (x @ router_w).astype(jnp.float32) + router_b.astype(jnp.float32)
    top_logit, top_e = lax.top_k(logits, TOP_K)
    top_w = jax.nn.softmax(top_logit, axis=-1)
    n_slots = T * TOP_K
    slot_e = top_e.reshape(-1)
    slot_tok = jnp.repeat(jnp.arange(T, dtype=jnp.int32), TOP_K)
    slot_w = top_w.reshape(-1)
    order = jnp.argsort(slot_e)
    e_sorted = slot_e[order]
    counts = jax.ops.segment_sum(jnp.ones((n_slots,), jnp.int32), slot_e, num_segments=N_EXPERTS)
    padded = (counts + MOE_BLOCK - 1) // MOE_BLOCK * MOE_BLOCK
    start = jnp.cumsum(counts) - counts
    pend = jnp.cumsum(padded)
    pstart = pend - padded
    dest = pstart[e_sorted] + jnp.arange(n_slots, dtype=jnp.int32) - start[e_sorted]
    n_blocks = -(-n_slots // MOE_BLOCK) + N_EXPERTS
    rows = n_blocks * MOE_BLOCK
    row_tok = jnp.full((rows,), T, jnp.int32).at[dest].set(slot_tok[order])
    row_w = jnp.zeros((rows,), jnp.float32).at[dest].set(slot_w[order])
    blk_e = jnp.minimum(jnp.searchsorted(pend, jnp.arange(n_blocks, dtype=jnp.int32) * MOE_BLOCK, side='right'),
                        N_EXPERTS - 1)
    x_pad = jnp.concatenate([x, jnp.zeros((1, x.shape[-1]), x.dtype)], axis=0)

    def expert_block(args):
        tok, e = args
        hid = x_pad[tok] @ up_w[e] + up_b[e]
        gate, up = jnp.split(hid, 2, axis=-1)
        gate = jnp.minimum(gate, SWIGLU_LIMIT)
        up = jnp.clip(up, -SWIGLU_LIMIT, SWIGLU_LIMIT)
        act = (up + 1) * gate * jax.nn.sigmoid(SWIGLU_ALPHA * gate)
        return act @ down_w[e] + down_b[e]

    out = lax.map(expert_block, (row_tok.reshape(n_blocks, MOE_BLOCK), blk_e))
    out = out.reshape(rows, -1) * row_w[:, None].astype(out.dtype)
    y = jax.ops.segment_sum(out, row_tok, num_segments=T + 1)[:T]
    return y.reshape(shp)


def setup_inputs(seed: int = 0) -> dict:
    key = jax.random.key(seed)
    keys = iter(jax.random.split(key, 48))

    def nrm(shape, scale):
        return jax.random.normal(next(keys), shape, jnp.float32) * scale

    D = D_MODEL
    la = [min(w, PAST_LEN) for w in A_WINDOWS]
    lb = min(B_WINDOW, PAST_LEN)
    return {
        'x_prompt': nrm((BATCH, SEQ, D), 1.0),
        'x_sample': nrm((DEC_BATCH, DEC_SEQ, D), 1.0),
        'c_prompt': nrm((BATCH, D), 1.0),
        'c_sample': nrm((DEC_BATCH, D), 1.0),
        'cache_a1_kv': nrm((DEPTH, DEC_BATCH, la[0], 2, A_HEADS, HEAD_DIM), 1.0),
        'cache_a2_kv': nrm((DEPTH, DEC_BATCH, la[1], 2, A_HEADS, HEAD_DIM), 1.0),
        'cache_a3_kv': nrm((DEPTH, DEC_BATCH, la[2], 2, A_HEADS, HEAD_DIM), 1.0),
        'cache_b_kv': nrm((DEPTH, DEC_BATCH, lb, 2, B_KV_HEADS, HEAD_DIM), 1.0),
        'state_c_mem': nrm((DEPTH, DEC_BATCH, C_HEADS, C_HEAD_DIM, C_HEAD_DIM), 0.05),
        'state_c_norm': jnp.abs(nrm((DEPTH, DEC_BATCH, C_HEADS, C_HEAD_DIM), 0.1)),
        'state_c_max': nrm((DEPTH, DEC_BATCH, C_HEADS), 1.0),
        'state_c_conv': nrm((DEPTH, DEC_BATCH, C_CONV - 1, 2 * C_WIDTH), 1.0),
        'rel_bias': nrm((N_BUCKETS, N_BIAS_HEADS), 0.2),
        'ada_w': nrm((DEPTH, D, 6 * D), D ** -0.5),
        'ada_b': nrm((DEPTH, 6 * D), 0.02),
        'in_w': nrm((DEPTH, D, D_IN), D ** -0.5),
        'conv_w': nrm((DEPTH, C_CONV, 2 * C_WIDTH), C_CONV ** -0.5),
        'conv_b': nrm((DEPTH, 2 * C_WIDTH), 0.02),
        'igate_b': nrm((DEPTH, C_HEADS), 0.1),
        'fgate_b': jnp.linspace(3.0, 6.0, C_HEADS, dtype=jnp.float32)[None, :] + nrm((DEPTH, C_HEADS), 0.1),
        'sink_b': nrm((DEPTH, B_Q_HEADS), 0.5),
        'br_a_w': nrm((DEPTH, A_WIDTH, D), A_WIDTH ** -0.5),
        'br_b_w': nrm((DEPTH, B_WIDTH, D), B_WIDTH ** -0.5),
        'br_c_w': nrm((DEPTH, C_WIDTH, D), C_WIDTH ** -0.5),
        'gate_w': nrm((DEPTH, D, 3 * D), D ** -0.5),
        'gate_b': nrm((DEPTH, 3 * D), 0.02),
        'out_w': nrm((DEPTH, D, D), D ** -0.5 * DN_BETA),
        'ln1_g': 1.0 + nrm((DEPTH, D), 0.02),
        'ln1_b': nrm((DEPTH, D), 0.02),
        'router_w': nrm((DEPTH, D, N_EXPERTS), D ** -0.5),
        'router_b': nrm((DEPTH, N_EXPERTS), 0.01),
        'up_w': nrm((DEPTH, N_EXPERTS, D, 2 * D_FF), D ** -0.5),
        'up_b': nrm((DEPTH, N_EXPERTS, 2 * D_FF), 0.02),
        'down_w': nrm((DEPTH, N_EXPERTS, D_FF, D), D_FF ** -0.5 * DN_BETA),
        'down_b': nrm((DEPTH, N_EXPERTS, D), 0.02),
        'ln2_g': 1.0 + nrm((DEPTH, D), 0.02),
        'ln2_b': nrm((DEPTH, D), 0.02),
    }


def reference(x_prompt, x_sample, c_prompt, c_sample, cache_a1_kv, cache_a2_kv, cache_a3_kv, cache_b_kv,
              state_c_mem, state_c_norm, state_c_max, state_c_conv, rel_bias, ada_w, ada_b, in_w, conv_w, conv_b,
              igate_b, fgate_b, sink_b, br_a_w, br_b_w, br_c_w, gate_w, gate_b, out_w, ln1_g, ln1_b,
              router_w, router_b, up_w, up_b, down_w, down_b, ln2_g, ln2_b):
    def run_group(x, c, caches):
        new = []
        for l in range(DEPTH):
            ada = jax.nn.silu(c) @ ada_w[l] + ada_b[l]
            sh1, sc1, g1, sh2, sc2, g2 = [t[:, None, :] for t in jnp.split(ada, 6, axis=-1)]
            u = x * (1 + sc1) + sh1
            layer_cache = None if caches is None else tuple(cc[l] for cc in caches)
            y, st = token_mixer(u, rel_bias, in_w[l], conv_w[l], conv_b[l], igate_b[l], fgate_b[l], sink_b[l],
                                br_a_w[l], br_b_w[l], br_c_w[l], gate_w[l], gate_b[l], out_w[l], layer_cache)
            x = layer_norm(DN_ALPHA * x + g1 * y, ln1_g[l], ln1_b[l])
            u = x * (1 + sc2) + sh2
            f = moe_ffn(u, router_w[l], router_b[l], up_w[l], up_b[l], down_w[l], down_b[l])
            x = layer_norm(DN_ALPHA * x + g2 * f, ln2_g[l], ln2_b[l])
            new.append(st)
        return x, [jnp.stack(t) for t in zip(*new)]

    y_prompt, sp = run_group(x_prompt, c_prompt, None)
    y_sample, ss = run_group(x_sample, c_sample, (cache_a1_kv, cache_a2_kv, cache_a3_kv, cache_b_kv,
                                                  state_c_mem, state_c_norm, state_c_max, state_c_conv))
    return (y_prompt, y_sample, sp[0], ss[0], sp[1], ss[1], sp[2], ss[2], sp[3], ss[3],
            sp[4], ss[4], sp[5], ss[5], sp[6], ss[6], sp[7], ss[7])
```

```python
import functools
import math

import jax
import jax.numpy as jnp
import numpy as np
from jax import lax
from jax.experimental import pallas as pl
from jax.experimental.pallas import tpu as pltpu

F32 = jnp.float32
BF16 = jnp.bfloat16

D_MODEL = 1024
DEPTH = 4
PAST_LEN = 8192
HEAD_DIM = 64
A_WINDOWS = (128, 512, 2048)
A_DILATIONS = (1, 4, 16)
A_GROUPS = 3
A_HEADS = 4
A_WIDTH = A_HEADS * HEAD_DIM
A_QKV = A_GROUPS * A_WIDTH
B_WINDOW = 128
B_Q_HEADS = 16
B_KV_HEADS = 2
B_REP = B_Q_HEADS // B_KV_HEADS
B_WIDTH = B_Q_HEADS * HEAD_DIM
B_KV_WIDTH = B_KV_HEADS * HEAD_DIM
C_HEADS = 4
C_HEAD_DIM = 256
C_WIDTH = C_HEADS * C_HEAD_DIM
C_CONV = 4
C_CHUNK = 128
N_BUCKETS = 32
MAX_DISTANCE = 2048
B_BIAS_OFF = A_GROUPS * A_HEADS
LOG_BUCKET_RATIO = math.log(MAX_DISTANCE / (N_BUCKETS // 2))
BAND = 128
N_EXPERTS = 32
TOP_K = 4
D_FF = 1024
SWIGLU_LIMIT = 7.0
SWIGLU_ALPHA = 1.702
DN_ALPHA = (2 * DEPTH) ** 0.25
LN_EPS = 1e-5
NEG = -1e30

OFF_BQ = 0
OFF_CQ = 1024
OFF_CK = 2048
OFF_CV = 3072
OFF_CO = 4096
OFF_AQ = 5120
OFF_AK = 5888
OFF_AV = 6656
OFF_BK = 7424
OFF_BV = 7552
C_PROJ = 7680
SRC_AQ, SRC_AK, SRC_AV = 0, 768, 1536
SRC_BQ, SRC_BK, SRC_BV = 2304, 3328, 3456
SRC_CQ, SRC_CK, SRC_CV = 3584, 4608, 5632
SRC_CI, SRC_CF, SRC_CO = 6656, 6660, 6664

LANE = 128
MOE_ROWS = 256
TOK_CHUNK = 4096
VMEM_LIMIT = 48 * 1024 * 1024


def _cparams(sem):
    return pltpu.CompilerParams(dimension_semantics=sem, vmem_limit_bytes=VMEM_LIMIT)


def _sigmoid(x):
    return 1.0 / (1.0 + jnp.exp(-x))


def _log_sigmoid(x):
    return jnp.minimum(x, 0.0) - jnp.log(1.0 + jnp.exp(-jnp.abs(x)))


def _silu(x):
    return x * _sigmoid(x)


def _dot(a, b):
    return jnp.dot(a, b, preferred_element_type=F32)


def _dot_nt(a, b):
    return lax.dot_general(a, b, (((1,), (1,)), ((), ())), preferred_element_type=F32)


def _dot_tn(a, b):
    return lax.dot_general(a, b, (((0,), (0,)), ((), ())), preferred_element_type=F32)


def _ada_kernel(c_ref, w_ref, b_ref, o_ref):
    a = _silu(c_ref[...]).astype(BF16)
    o_ref[0] = _dot(a, w_ref[0].astype(BF16)) + b_ref[0]


def ada_all(c, ada_w, ada_b):
    n, d = c.shape
    tn = 1536
    nt = ada_w.shape[2] // tn
    n_layers = ada_w.shape[0]
    return pl.pallas_call(
        _ada_kernel,
        grid=(n_layers, nt),
        in_specs=[pl.BlockSpec((n, d), lambda l, j: (0, 0)),
                  pl.BlockSpec((1, d, tn), lambda l, j: (l, 0, j)),
                  pl.BlockSpec((1, 1, tn), lambda l, j: (l, 0, j))],
        out_specs=pl.BlockSpec((1, n, tn), lambda l, j: (l, 0, j)),
        out_shape=jax.ShapeDtypeStruct((n_layers, n, ada_w.shape[2]), F32),
        compiler_params=_cparams(("parallel", "arbitrary")),
        name="ada",
    )(c, ada_w, ada_b.reshape(n_layers, 1, -1))


def _ada_spec(ada, comp, tm, rows_per_cond):
    if rows_per_cond > 1:
        tiles = rows_per_cond // tm
        return pl.BlockSpec((1, 1, D_MODEL), lambda i, *_: ((i // tiles) * 6 + comp, 0, 0))
    return pl.BlockSpec((tm, D_MODEL), lambda i, *_: (i, comp))


def _ada_val(ref, rows_per_cond):
    return ref[0] if rows_per_cond > 1 else ref[...]


def _ada_arg(ada_l, rows_per_cond):
    if rows_per_cond > 1:
        return ada_l.reshape(-1, 1, D_MODEL)
    return ada_l


def _inproj_kernel(x_ref, sh_ref, sc_ref, w_ref, wif_ref, bif_ref, o_ref, gif_ref, u_scr, *, rpc):
    @pl.when(pl.program_id(1) == 0)
    def _():
        u = x_ref[...] * (1.0 + _ada_val(sc_ref, rpc)) + _ada_val(sh_ref, rpc)
        ub = u.astype(BF16)
        u_scr[...] = ub
        gif_ref[...] = _dot(ub, wif_ref[...]) + bif_ref[...]

    o_ref[...] = _dot(u_scr[...], w_ref[...]).astype(o_ref.dtype)


def inproj(x, ada_l, w_main, w_if, b_if, rows_per_cond, out_dtype):
    m = x.shape[0]
    tm = min(m, 1024)
    tn = 512
    a = _ada_arg(ada_l, rows_per_cond)
    return pl.pallas_call(
        functools.partial(_inproj_kernel, rpc=rows_per_cond),
        grid=(m // tm, C_PROJ // tn),
        in_specs=[pl.BlockSpec((tm, D_MODEL), lambda i, j: (i, 0)),
                  _ada_spec(a, 0, tm, rows_per_cond),
                  _ada_spec(a, 1, tm, rows_per_cond),
                  pl.BlockSpec((D_MODEL, tn), lambda i, j: (0, j)),
                  pl.BlockSpec((D_MODEL, LANE), lambda i, j: (0, 0)),
                  pl.BlockSpec((1, LANE), lambda i, j: (0, 0))],
        out_specs=[pl.BlockSpec((tm, tn), lambda i, j: (i, j)),
                   pl.BlockSpec((tm, LANE), lambda i, j: (i, 0))],
        out_shape=[jax.ShapeDtypeStruct((m, C_PROJ), out_dtype),
                   jax.ShapeDtypeStruct((m, LANE), F32)],
        scratch_shapes=[pltpu.VMEM((tm, D_MODEL), BF16)],
        compiler_params=_cparams(("parallel", "arbitrary")),
        name="inproj",
    )(x, a, a, w_main, w_if, b_if)


def _rel_bucket(dist):
    exact = N_BUCKETS // 2
    d = jnp.maximum(dist, 1).astype(F32)
    far = exact + (jnp.log(d / exact) / LOG_BUCKET_RATIO * (N_BUCKETS - exact)).astype(jnp.int32)
    return jnp.where(dist < exact, dist, jnp.minimum(far, N_BUCKETS - 1))


def band_bias(table, dilation):
    t = jnp.arange(BAND)[:, None]
    s = jnp.arange(BAND)[None, :]
    out = []
    for dist in (BAND + t - s, t - s):
        valid = (dist >= 0) & (dist <= BAND)
        b = table[_rel_bucket(jnp.maximum(dist, 0) * dilation)]
        out.append(jnp.where(valid[None], b.transpose(2, 0, 1).astype(F32), NEG))
    return jnp.stack(out)


def cache_bias(table, dilation):
    j = BAND - jnp.arange(BAND)
    bc = table[_rel_bucket(j * dilation)].T.astype(F32)
    b0 = jnp.broadcast_to(table[_rel_bucket(jnp.zeros((1,), jnp.int32))].T.astype(F32), bc.shape)
    return bc, b0


def _band_kernel(*refs, n_heads, rep, with_sink, with_lse):
    q_ref, kc_ref, kp_ref, vc_ref, vp_ref, bias_ref = refs[:6]
    pos = 6
    sink_ref = None
    if with_sink:
        sink_ref = refs[pos]
        pos += 1
    o_ref = refs[pos]
    l_ref = refs[pos + 1] if with_lse else None
    has_prev = pl.program_id(2) > 0
    q = q_ref[0]
    kc, kp, vc, vp = kc_ref[0], kp_ref[0], vc_ref[0], vp_ref[0]
    for h in range(n_heads):
        qs = slice(h * HEAD_DIM, (h + 1) * HEAD_DIM)
        ks = slice((h // rep) * HEAD_DIM, (h // rep + 1) * HEAD_DIM)
        qh = q[:, qs]
        s_c = _dot_nt(qh, kc[:, ks]) * (HEAD_DIM ** -0.5) + bias_ref[1, h]
        s_p = _dot_nt(qh, kp[:, ks]) * (HEAD_DIM ** -0.5) + bias_ref[0, h]
        s_p = jnp.where(has_prev, s_p, NEG)
        m = jnp.maximum(jnp.max(s_c, axis=1, keepdims=True), jnp.max(s_p, axis=1, keepdims=True))
        p_c = jnp.exp(s_c - m)
        p_p = jnp.exp(s_p - m)
        l = jnp.sum(p_c, axis=1, keepdims=True) + jnp.sum(p_p, axis=1, keepdims=True)
        o = _dot(p_c.astype(BF16), vc[:, ks]) + _dot(p_p.astype(BF16), vp[:, ks])
        if with_sink:
            den = l + jnp.exp(sink_ref[h:h + 1, 0:1] - m)
        else:
            den = l
        o_ref[0, :, qs] = (o / den).astype(o_ref.dtype)
        if with_lse:
            l_ref[0, :, qs] = jnp.broadcast_to(m + jnp.log(l), (BAND, HEAD_DIM))


def band_attention(proj, bsz, seq, dilation, q_off, k_off, v_off, n_heads, n_kv, bias, sink, out_dtype):
    m = seq // dilation
    qw, kw = n_heads * HEAD_DIM, n_kv * HEAD_DIM
    p3 = proj.reshape(bsz, m, dilation * C_PROJ)
    with_sink = sink is not None
    with_lse = not with_sink

    def col(off, w):
        return lambda b, r, n: (b, n, (r * C_PROJ + off) // w)

    def col_prev(off, w):
        return lambda b, r, n: (b, jnp.maximum(n - 1, 0), (r * C_PROJ + off) // w)

    in_specs = [pl.BlockSpec((1, BAND, qw), col(q_off, qw)),
                pl.BlockSpec((1, BAND, kw), col(k_off, kw)),
                pl.BlockSpec((1, BAND, kw), col_prev(k_off, kw)),
                pl.BlockSpec((1, BAND, kw), col(v_off, kw)),
                pl.BlockSpec((1, BAND, kw), col_prev(v_off, kw)),
                pl.BlockSpec((2, n_heads, BAND, BAND), lambda b, r, n: (0, 0, 0, 0))]
    args = [p3, p3, p3, p3, p3, bias]
    if with_sink:
        in_specs.append(pl.BlockSpec((n_heads, LANE), lambda b, r, n: (0, 0)))
        args.append(sink)
    o_spec = pl.BlockSpec((1, BAND, qw), lambda b, r, n: (b, n, r))
    out_specs = [o_spec]
    out_shape = [jax.ShapeDtypeStruct((bsz, m, dilation * qw), out_dtype)]
    if with_lse:
        out_specs.append(o_spec)
        out_shape.append(jax.ShapeDtypeStruct((bsz, m, dilation * qw), F32))
    outs = pl.pallas_call(
        functools.partial(_band_kernel, n_heads=n_heads, rep=n_heads // n_kv,
                          with_sink=with_sink, with_lse=with_lse),
        grid=(bsz, dilation, m // BAND),
        in_specs=in_specs, out_specs=out_specs, out_shape=out_shape,
        compiler_params=_cparams(("parallel", "parallel", "arbitrary")),
        name="band_attn_d%d_h%d" % (dilation, n_heads),
    )(*args)
    return [o.reshape(bsz * seq, qw) for o in outs]


def _cache_attn_kernel(*refs, n_heads, rep, bt, with_sink, with_lse):
    q_ref, kn_ref, vn_ref, c_ref, bc_ref, b0_ref = refs[:6]
    pos = 6
    sink_ref = None
    if with_sink:
        sink_ref = refs[pos]
        pos += 1
    o_ref = refs[pos]
    l_ref = refs[pos + 1] if with_lse else None
    qw = n_heads * HEAD_DIM
    kw = (n_heads // rep) * HEAD_DIM
    hp = bc_ref.shape[0]
    hrow = lax.broadcasted_iota(jnp.int32, (hp, qw), 0)
    hcol = lax.broadcasted_iota(jnp.int32, (hp, qw), 1) // HEAD_DIM
    hmask = hrow == hcol
    bc = bc_ref[...]
    b0 = b0_ref[:, 0:1]

    def expand(x):
        parts = []
        for h in range(n_heads):
            kv = h // rep
            parts.append(x[:, kv * HEAD_DIM:(kv + 1) * HEAD_DIM])
        return jnp.concatenate(parts, axis=1)

    for b in range(bt):
        q = q_ref[b:b + 1, :].astype(F32) * (HEAD_DIM ** -0.5)
        qbd = jnp.where(hmask, jnp.broadcast_to(q, (hp, qw)), 0.0)
        cb = c_ref[b]
        kn = kn_ref[b:b + 1, :].astype(F32)
        vn = vn_ref[b:b + 1, :].astype(F32)
        if rep > 1:
            kc, vc = expand(cb[:, :kw]), expand(cb[:, kw:])
            kn, vn = expand(kn), expand(vn)
        else:
            kc, vc = cb[:, :kw], cb[:, kw:]
        s = _dot_nt(qbd.astype(BF16), kc.astype(BF16)) + bc
        s0 = jnp.sum(qbd * kn, axis=1, keepdims=True) + b0
        m = jnp.maximum(jnp.max(s, axis=1, keepdims=True), s0)
        p = jnp.exp(s - m)
        p0 = jnp.exp(s0 - m)
        l = jnp.sum(p, axis=1, keepdims=True) + p0
        o = _dot(p.astype(BF16), vc.astype(BF16)) + p0 * vn
        if with_sink:
            den = l + jnp.exp(sink_ref[:, 0:1] - m)
        else:
            den = l
        o = jnp.where(hmask, o / den, 0.0)
        o_ref[b:b + 1, :] = jnp.sum(o, axis=0, keepdims=True).astype(o_ref.dtype)
        if with_lse:
            lse = jnp.where(hmask, jnp.broadcast_to(m + jnp.log(l), (hp, qw)), 0.0)
            l_ref[b:b + 1, :] = jnp.sum(lse, axis=0, keepdims=True)


def cache_attention(proj, cache_l, dilation, q_off, k_off, v_off, n_heads, n_kv, bc, b0, sink, out_dtype):
    n, w, cw = cache_l.shape
    qw, kw = n_heads * HEAD_DIM, n_kv * HEAD_DIM
    bt = 8
    hp = max(8, n_heads)
    c3 = cache_l.reshape(n, w // dilation, dilation * cw)

    def pad_heads(a):
        return jnp.pad(a, ((0, hp - n_heads), (0, 0)))

    with_sink = sink is not None
    with_lse = not with_sink
    in_specs = [pl.BlockSpec((bt, qw), lambda i: (i, q_off // qw)),
                pl.BlockSpec((bt, kw), lambda i: (i, k_off // kw)),
                pl.BlockSpec((bt, kw), lambda i: (i, v_off // kw)),
                pl.BlockSpec((bt, BAND, cw), lambda i: (i, 0, 0)),
                pl.BlockSpec((hp, BAND), lambda i: (0, 0)),
                pl.BlockSpec((hp, BAND), lambda i: (0, 0))]
    args = [proj, proj, proj, c3, pad_heads(bc), pad_heads(b0)]
    if with_sink:
        in_specs.append(pl.BlockSpec((hp, LANE), lambda i: (0, 0)))
        args.append(pad_heads(sink))
    o_spec = pl.BlockSpec((bt, qw), lambda i: (i, 0))
    out_specs, out_shape = [o_spec], [jax.ShapeDtypeStruct((n, qw), out_dtype)]
    if with_lse:
        out_specs.append(o_spec)
        out_shape.append(jax.ShapeDtypeStruct((n, qw), F32))
    return pl.pallas_call(
        functools.partial(_cache_attn_kernel, n_heads=n_heads, rep=n_heads // n_kv, bt=bt,
                          with_sink=with_sink, with_lse=with_lse),
        grid=(n // bt,),
        in_specs=in_specs, out_specs=out_specs, out_shape=out_shape,
        compiler_params=_cparams(("parallel",)),
        name="cache_attn_d%d_h%d" % (dilation, n_heads),
    )(*args)


def _shift_rows(x, tail, k):
    sh = pltpu.roll(x, k, 0)
    first = jnp.where(lax.broadcasted_iota(jnp.int32, (8, x.shape[1]), 0) < k,
                      pltpu.roll(tail, k, 0), sh[0:8])
    return jnp.concatenate([first, sh[8:]], axis=0)


def _mlstm_chunk_kernel(q_ref, k_ref, v_ref, o_ref, g_ref, gt_ref, cw_ref, cb_ref,
                        h_ref, mem_ref, nrm_ref, max_ref,
                        mem_s, nrm_s, m_s, tail_s):
    c = pl.program_id(1)
    L = C_CHUNK

    @pl.when(c == 0)
    def _():
        mem_s[...] = jnp.zeros_like(mem_s)
        nrm_s[...] = jnp.zeros_like(nrm_s)
        m_s[...] = jnp.zeros_like(m_s)
        tail_s[...] = jnp.zeros_like(tail_s)

    def conv(x_ref_, which):
        x = x_ref_[0].astype(F32)
        lo = which * C_WIDTH
        tail = tail_s[which]
        y = cb_ref[:, lo:lo + C_WIDTH] + x * cw_ref[3:4, lo:lo + C_WIDTH]
        for k in (1, 2, 3):
            y = y + _shift_rows(x, tail, k) * cw_ref[3 - k:4 - k, lo:lo + C_WIDTH]
        tail_s[which] = x[L - 8:L]
        return _silu(y)

    qa = conv(q_ref, 0)
    ka = conv(k_ref, 1) * (C_HEAD_DIM ** -0.5)
    va = v_ref[0]
    g = g_ref[0]
    gt = gt_ref[0]
    t_i = lax.broadcasted_iota(jnp.int32, (L, L), 0)
    s_i = lax.broadcasted_iota(jnp.int32, (L, L), 1)
    tri = s_i <= t_i
    for h in range(C_HEADS):
        hs = slice(h * C_HEAD_DIM, (h + 1) * C_HEAD_DIM)
        q, k, v = qa[:, hs], ka[:, hs], va[:, hs]
        lf_col = _log_sigmoid(g[:, 4 + h:5 + h])
        lf_row = _log_sigmoid(gt[4 + h:5 + h, :])
        ig_col = g[:, h:h + 1]
        ig_row = gt[h:h + 1, :]
        a_col = jnp.sum(jnp.where(tri, lf_row, 0.0), axis=1, keepdims=True)
        a_row = jnp.sum(jnp.where(t_i <= s_i, lf_col, 0.0), axis=0, keepdims=True)
        a_last = jnp.sum(lf_row, axis=1, keepdims=True)
        m_prev = m_s[h:h + 1, 0:1]
        inter = a_col + m_prev
        dmat = jnp.where(tri, a_col - a_row + ig_row, NEG)
        mt = jnp.maximum(inter, jnp.max(dmat, axis=1, keepdims=True))
        w_inter = jnp.exp(inter - mt)
        qb, kb, vb = q.astype(BF16), k.astype(BF16), v.astype(BF16)
        s = _dot_nt(qb, kb) * jnp.exp(dmat - mt)
        mem = mem_s[h]
        nrm = nrm_s[h:h + 1, :]
        num = w_inter * _dot(qb, mem.astype(BF16)) + _dot(s.astype(BF16), vb)
        den = w_inter * jnp.sum(q * nrm, axis=1, keepdims=True) + jnp.sum(s, axis=1, keepdims=True)
        hid = num / jnp.maximum(jnp.abs(den), jnp.exp(-mt))
        og = _sigmoid(o_ref[0, :, hs].astype(F32))
        h_ref[0, :, hs] = (hid * og).astype(h_ref.dtype)
        m_new = mt[L - 1:L, :]
        decay = jnp.exp(a_last + m_prev - m_new)
        w_s = jnp.exp(a_last - a_col + ig_col - m_new)
        kw_ = k * w_s
        mem_s[h] = decay * mem + _dot_tn(kw_.astype(BF16), vb)
        nrm_s[h:h + 1, :] = decay * nrm + jnp.sum(kw_, axis=0, keepdims=True)
        m_s[h:h + 1, :] = jnp.broadcast_to(m_new, (1, LANE))

    @pl.when(c == pl.num_programs(1) - 1)
    def _():
        mem_ref[0] = mem_s[...]
        nrm_ref[0] = nrm_s[0:C_HEADS, :]
        max_ref[0] = m_s[...]


def mlstm_prompt(proj, gif, bsz, seq, conv_w, conv_b):
    nc = seq // C_CHUNK
    p3 = proj.reshape(bsz, seq, C_PROJ)
    g3 = gif.reshape(bsz, seq, LANE)
    gt3 = jnp.swapaxes(g3[:, :, :8], 1, 2)

    def col(j):
        return pl.BlockSpec((1, C_CHUNK, C_WIDTH), lambda b, c: (b, c, j))

    outs = pl.pallas_call(
        _mlstm_chunk_kernel,
        grid=(bsz, nc),
        in_specs=[col(OFF_CQ // C_WIDTH), col(OFF_CK // C_WIDTH), col(OFF_CV // C_WIDTH),
                  col(OFF_CO // C_WIDTH),
                  pl.BlockSpec((1, C_CHUNK, LANE), lambda b, c: (b, c, 0)),
                  pl.BlockSpec((1, 8, C_CHUNK), lambda b, c: (b, 0, c)),
                  pl.BlockSpec((C_CONV, 2 * C_WIDTH), lambda b, c: (0, 0)),
                  pl.BlockSpec((1, 2 * C_WIDTH), lambda b, c: (0, 0))],
        out_specs=[pl.BlockSpec((1, C_CHUNK, C_WIDTH), lambda b, c: (b, c, 0)),
                   pl.BlockSpec((1, C_HEADS, C_HEAD_DIM, C_HEAD_DIM), lambda b, c: (b, 0, 0, 0)),
                   pl.BlockSpec((1, C_HEADS, C_HEAD_DIM), lambda b, c: (b, 0, 0)),
                   pl.BlockSpec((1, 8, LANE), lambda b, c: (b, 0, 0))],
        out_shape=[jax.ShapeDtypeStruct((bsz, seq, C_WIDTH), BF16),
                   jax.ShapeDtypeStruct((bsz, C_HEADS, C_HEAD_DIM, C_HEAD_DIM), F32),
                   jax.ShapeDtypeStruct((bsz, C_HEADS, C_HEAD_DIM), F32),
                   jax.ShapeDtypeStruct((bsz, 8, LANE), F32)],
        scratch_shapes=[pltpu.VMEM((C_HEADS, C_HEAD_DIM, C_HEAD_DIM), F32),
                        pltpu.VMEM((8, C_HEAD_DIM), F32),
                        pltpu.VMEM((8, LANE), F32),
                        pltpu.VMEM((2, 8, C_WIDTH), F32)],
        compiler_params=_cparams(("parallel", "arbitrary")),
        name="mlstm_chunk",
    )(p3, p3, p3, p3, g3, gt3, conv_w, conv_b.reshape(1, -1))
    oc, mem, nrm, mx = outs
    return oc.reshape(bsz * seq, C_WIDTH), mem, nrm, mx[:, :C_HEADS, 0]


def _mlstm_step_kernel(q_ref, k_ref, v_ref, o_ref, g_ref, cw_ref, cb_ref, conv_ref, mem_ref, nrm_ref, max_ref,
                       h_ref, memo_ref, nrmo_ref, maxo_ref, convo_ref, *, bt):
    for b in range(bt):
        cv0 = conv_ref[b]
        xq = q_ref[b:b + 1, :].astype(F32)
        xk = k_ref[b:b + 1, :].astype(F32)
        x = jnp.concatenate([xq, xk], axis=1)
        y = (cb_ref[...] + cv0[0:1] * cw_ref[0:1] + cv0[1:2] * cw_ref[1:2] + cv0[2:3] * cw_ref[2:3]
             + x * cw_ref[3:4])
        y = _silu(y)
        convo_ref[b] = jnp.concatenate([cv0[1:3], x], axis=0)
        g = g_ref[b:b + 1, :]
        vrow = v_ref[b:b + 1, :]
        ogate = _sigmoid(o_ref[b:b + 1, :].astype(F32))
        mrow = max_ref[0, b:b + 1, :]
        hs_out, m_out = [], []
        for h in range(C_HEADS):
            hs = slice(h * C_HEAD_DIM, (h + 1) * C_HEAD_DIM)
            q = y[:, hs]
            k = y[:, C_WIDTH + h * C_HEAD_DIM:C_WIDTH + (h + 1) * C_HEAD_DIM] * (C_HEAD_DIM ** -0.5)
            v = vrow[:, hs]
            lf = _log_sigmoid(g[:, 4 + h:5 + h])
            ig = g[:, h:h + 1]
            m_prev = mrow[:, h:h + 1]
            inter = lf + m_prev
            mt = jnp.maximum(inter, ig)
            w_inter = jnp.exp(inter - mt)
            qb = jnp.broadcast_to(q, (16, C_HEAD_DIM)).astype(BF16)
            kb = k.astype(BF16)
            s = jnp.sum(qb[0:1].astype(F32) * kb.astype(F32), axis=1, keepdims=True) * jnp.exp(ig - mt)
            mem = mem_ref[b, h]
            nrm = nrm_ref[b, h:h + 1, :]
            qm = _dot(qb, mem.astype(BF16))[0:1]
            num = w_inter * qm + s * v.astype(F32)
            den = w_inter * jnp.sum(q * nrm, axis=1, keepdims=True) + s
            hid = num / jnp.maximum(jnp.abs(den), jnp.exp(-mt))
            hs_out.append(hid)
            decay = jnp.exp(inter - mt)
            w_s = jnp.exp(ig - mt)
            kw_ = k * w_s
            row0 = lax.broadcasted_iota(jnp.int32, (16, C_HEAD_DIM), 0) == 0
            k16 = jnp.where(row0, jnp.broadcast_to(kw_, (16, C_HEAD_DIM)), 0.0).astype(BF16)
            v16 = jnp.broadcast_to(v, (16, C_HEAD_DIM)).astype(BF16)
            memo_ref[b, h] = decay * mem + _dot_tn(k16, v16)
            nrmo_ref[b, h:h + 1, :] = decay * nrm + kw_
            m_out.append(mt)
        h_ref[b:b + 1, :] = (jnp.concatenate(hs_out, axis=1) * ogate).astype(h_ref.dtype)
        maxo_ref[0, b:b + 1, :] = jnp.concatenate(m_out, axis=1)


def mlstm_step(proj, gif, conv_w, conv_b, conv0, mem0, nrm0, max0):
    n = proj.shape[0]
    bt = 8

    def col(j):
        return pl.BlockSpec((bt, C_WIDTH), lambda i: (i, j))

    state_specs = [pl.BlockSpec((bt, C_CONV - 1, 2 * C_WIDTH), lambda i: (i, 0, 0)),
                   pl.BlockSpec((bt, C_HEADS, C_HEAD_DIM, C_HEAD_DIM), lambda i: (i, 0, 0, 0)),
                   pl.BlockSpec((bt, C_HEADS, C_HEAD_DIM), lambda i: (i, 0, 0)),
                   pl.BlockSpec((1, bt, C_HEADS), lambda i: (i, 0, 0))]
    outs = pl.pallas_call(
        functools.partial(_mlstm_step_kernel, bt=bt),
        grid=(n // bt,),
        in_specs=[col(OFF_CQ // C_WIDTH), col(OFF_CK // C_WIDTH), col(OFF_CV // C_WIDTH),
                  col(OFF_CO // C_WIDTH),
                  pl.BlockSpec((bt, LANE), lambda i: (i, 0)),
                  pl.BlockSpec((C_CONV, 2 * C_WIDTH), lambda i: (0, 0)),
                  pl.BlockSpec((1, 2 * C_WIDTH), lambda i: (0, 0))] + state_specs,
        out_specs=[pl.BlockSpec((bt, C_WIDTH), lambda i: (i, 0)),
                   state_specs[1], state_specs[2], state_specs[3], state_specs[0]],
        out_shape=[jax.ShapeDtypeStruct((n, C_WIDTH), F32),
                   jax.ShapeDtypeStruct(mem0.shape, F32),
                   jax.ShapeDtypeStruct(nrm0.shape, F32),
                   jax.ShapeDtypeStruct((n // bt, bt, C_HEADS), F32),
                   jax.ShapeDtypeStruct(conv0.shape, F32)],
        compiler_params=_cparams(("parallel",)),
        name="mlstm_step",
    )(proj, proj, proj, proj, gif, conv_w, conv_b.reshape(1, -1), conv0, mem0, nrm0,
      max0.reshape(n // bt, bt, C_HEADS))
    oc, mem1, nrm1, max1, conv1 = outs
    return oc, mem1, nrm1, max1.reshape(n, C_HEADS), conv1


def _layer_norm(z, g, b):
    mu = jnp.mean(z, axis=1, keepdims=True)
    zc = z - mu
    var = jnp.mean(zc * zc, axis=1, keepdims=True)
    return zc * lax.rsqrt(var + LN_EPS) * g + b


def _merge_kernel(x_ref, sh1_ref, sc1_ref, g1_ref, sh2_ref, sc2_ref,
                  o1_ref, o2_ref, o3_ref, l1_ref, l2_ref, l3_ref, ob_ref, oc_ref,
                  gw_ref, gb_ref, wa_ref, wb_ref, wc_ref, wo_ref, lg_ref, lb_ref, rwh_ref, rwl_ref, rb_ref,
                  x1_ref, u2_ref, te_ref, tw_ref, *, rpc):
    D = D_MODEL
    x = x_ref[...]
    u = (x * (1.0 + _ada_val(sc1_ref, rpc)) + _ada_val(sh1_ref, rpc)).astype(BF16)
    l1, l2, l3 = l1_ref[...], l2_ref[...], l3_ref[...]
    lm = jnp.maximum(jnp.maximum(l1, l2), l3)
    e1, e2, e3 = jnp.exp(l1 - lm), jnp.exp(l2 - lm), jnp.exp(l3 - lm)
    oa = (e1 * o1_ref[...] + e2 * o2_ref[...] + e3 * o3_ref[...]) / (e1 + e2 + e3)
    merged = None
    for idx, (lhs, w_ref) in enumerate(((oa.astype(BF16), wa_ref), (ob_ref[...].astype(BF16), wb_ref),
                                          (oc_ref[...].astype(BF16), wc_ref))):
        gate = _sigmoid(_dot(u, gw_ref[:, idx * D:(idx + 1) * D]) + gb_ref[:, idx * D:(idx + 1) * D])
        term = gate * _dot(lhs, w_ref[...])
        merged = term if merged is None else merged + term
    y = _dot(merged.astype(BF16), wo_ref[...])
    x1 = _layer_norm(DN_ALPHA * x + _ada_val(g1_ref, rpc) * y, lg_ref[...], lb_ref[...])
    x1_ref[...] = x1
    u2 = x1 * (1.0 + _ada_val(sc2_ref, rpc)) + _ada_val(sh2_ref, rpc)
    u2_ref[...] = u2
    uh = u2.astype(BF16)
    ul = (u2 - uh.astype(F32)).astype(BF16)
    logits = _dot(uh, rwh_ref[...]) + _dot(uh, rwl_ref[...]) + _dot(ul, rwh_ref[...]) + rb_ref[...]
    lane = lax.broadcasted_iota(jnp.int32, logits.shape, 1)
    lanef = lane.astype(F32)
    logits = jnp.where(lane < N_EXPERTS, logits, -jnp.inf)
    te = jnp.zeros(logits.shape, F32)
    tw = jnp.zeros(logits.shape, F32)
    top0 = None
    for k in range(TOP_K):
        mx = jnp.max(logits, axis=1, keepdims=True)
        idx = jnp.min(jnp.where(logits == mx, lanef, float(LANE)), axis=1, keepdims=True)
        if top0 is None:
            top0 = mx
        te = jnp.where(lane == k, idx, te)
        tw = jnp.where(lane == k, jnp.exp(mx - top0), tw)
        logits = jnp.where(lanef == idx, -jnp.inf, logits)
    tw = tw / jnp.sum(tw, axis=1, keepdims=True)
    te_ref[...] = te.astype(jnp.int32)
    tw_ref[...] = tw


def merge(x, ada_l, o_a, l_a, ob, oc, gate_w, gate_b, wa, wb, wc, wo, ln_g, ln_b, rw_hi, rw_lo, rb,
          rows_per_cond):
    m = x.shape[0]
    tm = min(m, 256)
    D = D_MODEL
    a = _ada_arg(ada_l, rows_per_cond)

    def rows(w):
        return pl.BlockSpec((tm, w), lambda i: (i, 0))

    def full(shape):
        return pl.BlockSpec(shape, lambda i: (0,) * len(shape))

    in_specs = ([rows(D)] + [_ada_spec(a, c, tm, rows_per_cond) for c in (0, 1, 2, 3, 4)]
                + [rows(A_WIDTH)] * 6 + [rows(B_WIDTH), rows(C_WIDTH)]
                + [full((D, 3 * D)), full((1, 3 * D)), full((A_WIDTH, D)), full((B_WIDTH, D)),
                   full((C_WIDTH, D)), full((D, D)), full((1, D)), full((1, D)),
                   full((D, LANE)), full((D, LANE)), full((1, LANE))])
    return pl.pallas_call(
        functools.partial(_merge_kernel, rpc=rows_per_cond),
        grid=(m // tm,),
        in_specs=in_specs,
        out_specs=[rows(D), rows(D), rows(LANE), rows(LANE)],
        out_shape=[jax.ShapeDtypeStruct((m, D), F32), jax.ShapeDtypeStruct((m, D), F32),
                   jax.ShapeDtypeStruct((m, LANE), jnp.int32), jax.ShapeDtypeStruct((m, LANE), F32)],
        compiler_params=_cparams(("parallel",)),
        name="merge",
    )(x, a, a, a, a, a, *o_a, *l_a, ob, oc, gate_w, gate_b, wa, wb, wc, wo, ln_g, ln_b, rw_hi, rw_lo, rb)


def moe_plan(top_e, n_rows_block):
    t = top_e.shape[0]
    n_slots = t * TOP_K
    slot_e = top_e.reshape(-1)
    onehot = (slot_e[:, None] == jnp.arange(N_EXPERTS, dtype=jnp.int32)[None, :]).astype(jnp.int32)
    csum = jnp.cumsum(onehot, axis=0)
    counts = csum[-1]
    rank = jnp.sum((csum - onehot) * onehot, axis=1)
    padded = (counts + n_rows_block - 1) // n_rows_block * n_rows_block
    pend = jnp.cumsum(padded)
    pstart = pend - padded
    dest = (jnp.sum(onehot * pstart[None, :], axis=1) + rank).astype(jnp.int32)
    n_blocks = -(-n_slots // n_rows_block) + N_EXPERTS
    blk_start = jnp.arange(n_blocks, dtype=jnp.int32) * n_rows_block
    blk_e = jnp.minimum(jnp.sum((blk_start[:, None] >= pend[None, :]).astype(jnp.int32), axis=1),
                        N_EXPERTS - 1).astype(jnp.int32)
    n_used = (pend[-1:] // n_rows_block).astype(jnp.int32)
    return dest, blk_e, n_used, n_blocks


def _scatter_kernel(dest_ref, u_ref, xs_in_ref, xs_ref, sem, *, tt):
    del xs_in_ref
    base = pl.program_id(0) * tt * TOP_K

    def copy(r, k):
        d = dest_ref[base + r * TOP_K + k]
        return pltpu.make_async_copy(u_ref.at[pl.ds(r, 1)], xs_ref.at[pl.ds(d, 1)], sem)

    def issue(r, carry):
        for k in range(TOP_K):
            copy(r, k).start()
        return carry

    def drain(r, carry):
        for k in range(TOP_K):
            copy(r, k).wait()
        return carry

    lax.fori_loop(0, tt, issue, 0)
    lax.fori_loop(0, tt, drain, 0)


def moe_scatter(u2, dest, xs):
    t = u2.shape[0]
    tt = min(t, 128)
    for c0 in range(0, t, TOK_CHUNK):
        nt = min(TOK_CHUNK, t - c0)
        xs = pl.pallas_call(
            functools.partial(_scatter_kernel, tt=tt),
            grid_spec=pltpu.PrefetchScalarGridSpec(
                num_scalar_prefetch=1,
                grid=(nt // tt,),
                in_specs=[pl.BlockSpec((tt, D_MODEL), lambda i, d: (i, 0)),
                          pl.BlockSpec(memory_space=pl.ANY)],
                out_specs=pl.BlockSpec(memory_space=pl.ANY),
                scratch_shapes=[pltpu.SemaphoreType.DMA]),
            out_shape=jax.ShapeDtypeStruct(xs.shape, xs.dtype),
            input_output_aliases={2: 0},
            compiler_params=_cparams(("arbitrary",)),
            name="moe_scatter",
        )(lax.slice_in_dim(dest, c0 * TOP_K, (c0 + nt) * TOP_K), lax.slice_in_dim(u2, c0, c0 + nt), xs)
    return xs


def _expert_kernel(be_ref, nu_ref, x_ref, uw_ref, ub_ref, dw_ref, db_ref, o_ref):
    del be_ref
    i = pl.program_id(0)

    @pl.when(i < nu_ref[0])
    def _():
        hid = _dot(x_ref[...].astype(BF16), uw_ref[0]) + ub_ref[0]
        gate = jnp.minimum(hid[:, :D_FF], SWIGLU_LIMIT)
        up = jnp.clip(hid[:, D_FF:], -SWIGLU_LIMIT, SWIGLU_LIMIT)
        act = (up + 1.0) * gate * _sigmoid(SWIGLU_ALPHA * gate)
        o_ref[...] = _dot(act.astype(BF16), dw_ref[0]) + db_ref[0]

    @pl.when(i >= nu_ref[0])
    def _():
        o_ref[...] = jnp.zeros_like(o_ref)


def moe_experts(xs, blk_e, n_used, up_w, up_b, down_w, down_b):
    rows = xs.shape[0]
    mb = MOE_ROWS
    return pl.pallas_call(
        _expert_kernel,
        grid_spec=pltpu.PrefetchScalarGridSpec(
            num_scalar_prefetch=2,
            grid=(rows // mb,),
            in_specs=[pl.BlockSpec((mb, D_MODEL), lambda i, be, nu: (i, 0)),
                      pl.BlockSpec((1, D_MODEL, 2 * D_FF), lambda i, be, nu: (be[i], 0, 0)),
                      pl.BlockSpec((1, 1, 2 * D_FF), lambda i, be, nu: (be[i], 0, 0)),
                      pl.BlockSpec((1, D_FF, D_MODEL), lambda i, be, nu: (be[i], 0, 0)),
                      pl.BlockSpec((1, 1, D_MODEL), lambda i, be, nu: (be[i], 0, 0))],
            out_specs=pl.BlockSpec((mb, D_MODEL), lambda i, be, nu: (i, 0))),
        out_shape=jax.ShapeDtypeStruct((rows, D_MODEL), F32),
        compiler_params=_cparams(("arbitrary",)),
        name="moe_experts",
    )(blk_e, n_used, xs, up_w, up_b.reshape(N_EXPERTS, 1, -1), down_w, down_b.reshape(N_EXPERTS, 1, -1))


def _combine_kernel(dest_ref, rows_ref, tw_ref, x_ref, g2_ref, lg_ref, lb_ref, o_ref, buf, sem, *, tt, rpc):
    i = pl.program_id(0)
    n = pl.num_programs(0)

    def copy(step, slot, r, k):
        d = dest_ref[(step * tt + r) * TOP_K + k]
        return pltpu.make_async_copy(rows_ref.at[pl.ds(d, 1)], buf.at[slot, k, pl.ds(r, 1)], sem.at[slot])

    def issue(step, slot):
        def body(r, carry):
            for k in range(TOP_K):
                copy(step, slot, r, k).start()
            return carry
        lax.fori_loop(0, tt, body, 0)

    def drain(step, slot):
        def body(r, carry):
            for k in range(TOP_K):
                copy(step, slot, r, k).wait()
            return carry
        lax.fori_loop(0, tt, body, 0)

    slot = lax.rem(i, 2)

    @pl.when(i == 0)
    def _():
        issue(0, 0)

    @pl.when(i + 1 < n)
    def _():
        issue(i + 1, 1 - slot)

    drain(i, slot)
    tw = tw_ref[...]
    f = None
    for k in range(TOP_K):
        term = buf[slot, k] * tw[:, k:k + 1]
        f = term if f is None else f + term
    z = DN_ALPHA * x_ref[...] + _ada_val(g2_ref, rpc) * f
    o_ref[...] = _layer_norm(z, lg_ref[...], lb_ref[...])


def moe_combine(rows, dest, top_w, x1, ada_l, ln_g, ln_b, rows_per_cond):
    t = x1.shape[0]
    tt = min(t, 64)
    a = _ada_arg(ada_l, rows_per_cond)
    outs = []
    for c0 in range(0, t, TOK_CHUNK):
        nt = min(TOK_CHUNK, t - c0)
        tile0 = c0 // tt
        if rows_per_cond > 1:
            tiles = rows_per_cond // tt
            g2_spec = pl.BlockSpec((1, 1, D_MODEL), lambda i, d, tile0=tile0, tiles=tiles: (((i + tile0) // tiles) * 6 + 5, 0, 0))
        else:
            g2_spec = pl.BlockSpec((tt, D_MODEL), lambda i, d, tile0=tile0: (i + tile0, 5))
        outs.append(pl.pallas_call(
            functools.partial(_combine_kernel, tt=tt, rpc=rows_per_cond),
            grid_spec=pltpu.PrefetchScalarGridSpec(
                num_scalar_prefetch=1,
                grid=(nt // tt,),
                in_specs=[pl.BlockSpec(memory_space=pl.ANY),
                          pl.BlockSpec((tt, LANE), lambda i, d, tile0=tile0: (i + tile0, 0)),
                          pl.BlockSpec((tt, D_MODEL), lambda i, d, tile0=tile0: (i + tile0, 0)),
                          g2_spec,
                          pl.BlockSpec((1, D_MODEL), lambda i, d: (0, 0)),
                          pl.BlockSpec((1, D_MODEL), lambda i, d: (0, 0))],
                out_specs=pl.BlockSpec((tt, D_MODEL), lambda i, d: (i, 0)),
                scratch_shapes=[pltpu.VMEM((2, TOP_K, tt, D_MODEL), F32),
                                pltpu.SemaphoreType.DMA((2,))]),
            out_shape=jax.ShapeDtypeStruct((nt, D_MODEL), F32),
            compiler_params=_cparams(("arbitrary",)),
            name="moe_combine",
        )(lax.slice_in_dim(dest, c0 * TOP_K, (c0 + nt) * TOP_K), rows, top_w, x1, a, ln_g, ln_b))
    return outs[0] if len(outs) == 1 else jnp.concatenate(outs, axis=0)


def _roll_kernel(*refs, n):
    caches, news, outs, sem = refs[:n], refs[n:2 * n], refs[2 * n:3 * n], refs[3 * n]
    copies = []
    for j in range(n):
        w = caches[j].shape[2]
        copies.append(pltpu.make_async_copy(caches[j].at[:, :, pl.ds(1, w - 1)],
                                            outs[j].at[:, :, pl.ds(0, w - 1)], sem.at[2 * j]))
        copies.append(pltpu.make_async_copy(news[j], outs[j].at[:, :, pl.ds(w - 1, 1)], sem.at[2 * j + 1]))
    for c in copies:
        c.start()
    for c in copies:
        c.wait()


def roll_caches(caches, news):
    n = len(caches)
    return pl.pallas_call(
        functools.partial(_roll_kernel, n=n),
        in_specs=[pl.BlockSpec(memory_space=pl.ANY)] * (2 * n),
        out_specs=[pl.BlockSpec(memory_space=pl.ANY)] * n,
        out_shape=[jax.ShapeDtypeStruct(c.shape, c.dtype) for c in caches],
        scratch_shapes=[pltpu.SemaphoreType.DMA((2 * n,))],
        name="roll_caches",
    )(*caches, *news)


def _prep_weights(l, p):
    w = p["in_w"][l]
    order = [(SRC_BQ, B_WIDTH), (SRC_CQ, C_WIDTH), (SRC_CK, C_WIDTH), (SRC_CV, C_WIDTH), (SRC_CO, C_WIDTH),
             (SRC_AQ, A_QKV), (SRC_AK, A_QKV), (SRC_AV, A_QKV), (SRC_BK, B_KV_WIDTH), (SRC_BV, B_KV_WIDTH)]
    w_main = jnp.concatenate([w[:, s:s + n] for s, n in order], axis=1).astype(BF16)
    w_if = jnp.pad(w[:, SRC_CI:SRC_CI + 2 * C_HEADS], ((0, 0), (0, LANE - 2 * C_HEADS))).astype(BF16)
    b_if = jnp.pad(jnp.concatenate([p["igate_b"][l], p["fgate_b"][l]]), (0, LANE - 2 * C_HEADS)).reshape(1, LANE)
    rw = jnp.pad(p["router_w"][l], ((0, 0), (0, LANE - N_EXPERTS)))
    rw_hi = rw.astype(BF16)
    rw_lo = (rw - rw_hi.astype(F32)).astype(BF16)
    rb = jnp.pad(p["router_b"][l], (0, LANE - N_EXPERTS)).reshape(1, LANE)
    return dict(
        w_main=w_main, w_if=w_if, b_if=b_if,
        gate_w=p["gate_w"][l].astype(BF16), gate_b=p["gate_b"][l].reshape(1, -1),
        wa=p["br_a_w"][l].astype(BF16), wb=p["br_b_w"][l].astype(BF16), wc=p["br_c_w"][l].astype(BF16),
        wo=p["out_w"][l].astype(BF16),
        ln1_g=p["ln1_g"][l].reshape(1, -1), ln1_b=p["ln1_b"][l].reshape(1, -1),
        ln2_g=p["ln2_g"][l].reshape(1, -1), ln2_b=p["ln2_b"][l].reshape(1, -1),
        rw_hi=rw_hi, rw_lo=rw_lo, rb=rb,
        up_w=p["up_w"][l].astype(BF16), up_b=p["up_b"][l],
        down_w=p["down_w"][l].astype(BF16), down_b=p["down_b"][l],
        conv_w=p["conv_w"][l], conv_b=p["conv_b"][l],
        sink=jnp.broadcast_to(p["sink_b"][l][:, None], (B_Q_HEADS, LANE)),
    )


def _moe_layer(x1, u2, top_e, top_w, ada_l, wl, xs, rows_per_cond):
    dest, blk_e, n_used, n_blocks = moe_plan(top_e[:, :TOP_K], MOE_ROWS)
    if xs is None:
        xs = jnp.zeros((n_blocks * MOE_ROWS, D_MODEL), F32)
    xs = moe_scatter(u2, dest, xs)
    rows = moe_experts(xs, blk_e, n_used, wl["up_w"], wl["up_b"], wl["down_w"], wl["down_b"])
    x2 = moe_combine(rows, dest, top_w, x1, ada_l, wl["ln2_g"], wl["ln2_b"], rows_per_cond)
    return x2, xs


def kernel(x_prompt, x_sample, c_prompt, c_sample, cache_a1_kv, cache_a2_kv, cache_a3_kv, cache_b_kv,
           state_c_mem, state_c_norm, state_c_max, state_c_conv, rel_bias, ada_w, ada_b, in_w, conv_w, conv_b,
           igate_b, fgate_b, sink_b, br_a_w, br_b_w, br_c_w, gate_w, gate_b, out_w, ln1_g, ln1_b,
           router_w, router_b, up_w, up_b, down_w, down_b, ln2_g, ln2_b):
    p = dict(in_w=in_w, conv_w=conv_w, conv_b=conv_b, igate_b=igate_b, fgate_b=fgate_b, sink_b=sink_b,
             br_a_w=br_a_w, br_b_w=br_b_w, br_c_w=br_c_w, gate_w=gate_w, gate_b=gate_b, out_w=out_w,
             ln1_g=ln1_g, ln1_b=ln1_b, router_w=router_w, router_b=router_b, up_w=up_w, up_b=up_b,
             down_w=down_w, down_b=down_b, ln2_g=ln2_g, ln2_b=ln2_b)
    bsz, seq, d = x_prompt.shape
    nb = x_sample.shape[0]
    n_layers = in_w.shape[0]
    ada = ada_all(jnp.concatenate([c_prompt, c_sample], axis=0), ada_w, ada_b)
    a_caches = (cache_a1_kv, cache_a2_kv, cache_a3_kv)
    band_tabs = [band_bias(rel_bias[:, g * A_HEADS:(g + 1) * A_HEADS], A_DILATIONS[g]) for g in range(A_GROUPS)]
    band_tab_b = band_bias(rel_bias[:, B_BIAS_OFF:B_BIAS_OFF + B_Q_HEADS], 1)
    cache_tabs = [cache_bias(rel_bias[:, g * A_HEADS:(g + 1) * A_HEADS], A_DILATIONS[g]) for g in range(A_GROUPS)]
    cache_tab_b = cache_bias(rel_bias[:, B_BIAS_OFF:B_BIAS_OFF + B_Q_HEADS], 1)

    xp = x_prompt.reshape(bsz * seq, d)
    xsm = x_sample.reshape(nb, d)
    xs_p = xs_s = None
    st_p = [[] for _ in range(8)]
    st_s = [[] for _ in range(4)]
    new_rows = [[] for _ in range(4)]
    for l in range(n_layers):
        wl = _prep_weights(l, p)
        ada_p = ada[l, :bsz]
        proj, gif = inproj(xp, ada_p, wl["w_main"], wl["w_if"], wl["b_if"], seq, BF16)
        o_a, l_a = [], []
        for g in range(A_GROUPS):
            o, lse = band_attention(proj, bsz, seq, A_DILATIONS[g], OFF_AQ + g * A_WIDTH, OFF_AK + g * A_WIDTH,
                                    OFF_AV + g * A_WIDTH, A_HEADS, A_HEADS, band_tabs[g], None, F32)
            o_a.append(o)
            l_a.append(lse)
        (ob,) = band_attention(proj, bsz, seq, 1, OFF_BQ, OFF_BK, OFF_BV, B_Q_HEADS, B_KV_HEADS,
                               band_tab_b, wl["sink"], BF16)
        oc, mem1, nrm1, max1 = mlstm_prompt(proj, gif, bsz, seq, wl["conv_w"], wl["conv_b"])
        x1, u2, top_e, top_w = merge(xp, ada_p, o_a, l_a, ob, oc, wl["gate_w"], wl["gate_b"], wl["wa"], wl["wb"],
                                     wl["wc"], wl["wo"], wl["ln1_g"], wl["ln1_b"], wl["rw_hi"], wl["rw_lo"],
                                     wl["rb"], seq)
        xp, xs_p = _moe_layer(x1, u2, top_e, top_w, ada_p, wl, xs_p, seq)
        p3 = proj.reshape(bsz, seq, C_PROJ)
        for g in range(A_GROUPS):
            wnd = min(A_WINDOWS[g], seq)
            kk = p3[:, seq - wnd:, OFF_AK + g * A_WIDTH:OFF_AK + (g + 1) * A_WIDTH]
            vv = p3[:, seq - wnd:, OFF_AV + g * A_WIDTH:OFF_AV + (g + 1) * A_WIDTH]
            st_p[g].append(jnp.stack([kk, vv], axis=2).astype(F32).reshape(bsz, wnd, 2, A_HEADS, HEAD_DIM))
        wnd = min(B_WINDOW, seq)
        kk = p3[:, seq - wnd:, OFF_BK:OFF_BK + B_KV_WIDTH]
        vv = p3[:, seq - wnd:, OFF_BV:OFF_BV + B_KV_WIDTH]
        st_p[3].append(jnp.stack([kk, vv], axis=2).astype(F32).reshape(bsz, wnd, 2, B_KV_HEADS, HEAD_DIM))
        st_p[4].append(mem1)
        st_p[5].append(nrm1)
        st_p[6].append(max1)
        st_p[7].append(p3[:, seq - (C_CONV - 1):, OFF_CQ:OFF_CQ + 2 * C_WIDTH].astype(F32))
        ada_s = ada[l, bsz:]
        proj_s, gif_s = inproj(xsm, ada_s, wl["w_main"], wl["w_if"], wl["b_if"], 1, F32)
        o_a, l_a = [], []
        for g in range(A_GROUPS):
            cache_l = a_caches[g][l].reshape(nb, -1, 2 * A_WIDTH)
            o, lse = cache_attention(proj_s, cache_l, A_DILATIONS[g], OFF_AQ + g * A_WIDTH, OFF_AK + g * A_WIDTH,
                                     OFF_AV + g * A_WIDTH, A_HEADS, A_HEADS, cache_tabs[g][0], cache_tabs[g][1],
                                     None, F32)
            o_a.append(o)
            l_a.append(lse)
            new_rows[g].append(jnp.concatenate(
                [proj_s[:, OFF_AK + g * A_WIDTH:OFF_AK + (g + 1) * A_WIDTH],
                 proj_s[:, OFF_AV + g * A_WIDTH:OFF_AV + (g + 1) * A_WIDTH]], axis=1).astype(F32))
        cache_l = cache_b_kv[l].reshape(nb, -1, 2 * B_KV_WIDTH)
        (ob,) = cache_attention(proj_s, cache_l, 1, OFF_BQ, OFF_BK, OFF_BV, B_Q_HEADS, B_KV_HEADS,
                                cache_tab_b[0], cache_tab_b[1], wl["sink"], BF16)
        new_rows[3].append(proj_s[:, OFF_BK:OFF_BK + 2 * B_KV_WIDTH].astype(F32))
        oc, mem1, nrm1, max1, conv1 = mlstm_step(proj_s, gif_s, wl["conv_w"], wl["conv_b"], state_c_conv[l],
                                                 state_c_mem[l], state_c_norm[l], state_c_max[l])
        x1, u2, top_e, top_w = merge(xsm, ada_s, o_a, l_a, ob, oc, wl["gate_w"], wl["gate_b"], wl["wa"], wl["wb"],
                                     wl["wc"], wl["wo"], wl["ln1_g"], wl["ln1_b"], wl["rw_hi"], wl["rw_lo"],
                                     wl["rb"], 1)
        xsm, xs_s = _moe_layer(x1, u2, top_e, top_w, ada_s, wl, xs_s, 1)
        st_s[0].append(mem1)
        st_s[1].append(nrm1)
        st_s[2].append(max1)
        st_s[3].append(conv1)

    caches = (cache_a1_kv, cache_a2_kv, cache_a3_kv, cache_b_kv)
    news = [jnp.stack(r).reshape((n_layers, nb, 1) + c.shape[3:]) for r, c in zip(new_rows, caches)]
    a1s, a2s, a3s, bs = roll_caches(caches, news)
    sp = [jnp.stack(t) for t in st_p]
    ss = [jnp.stack(t) for t in st_s]
    return (xp.reshape(bsz, seq, d), xsm.reshape(nb, 1, d),
            sp[0], a1s, sp[1], a2s, sp[2], a3s, sp[3], bs,
            sp[4], ss[0], sp[5], ss[1], sp[6], ss[2], sp[7], ss[3])
```

```python
import functools
import math

import jax
import jax.numpy as jnp
import numpy as np
from jax import lax
from jax.experimental import pallas as pl
from jax.experimental.pallas import tpu as pltpu

F32 = jnp.float32
BF16 = jnp.bfloat16

D_MODEL = 1024
DEPTH = 4
PAST_LEN = 8192
HEAD_DIM = 64
A_WINDOWS = (128, 512, 2048)
A_DILATIONS = (1, 4, 16)
A_GROUPS = 3
A_HEADS = 4
A_WIDTH = A_HEADS * HEAD_DIM
A_QKV = A_GROUPS * A_WIDTH
B_WINDOW = 128
B_Q_HEADS = 16
B_KV_HEADS = 2
B_REP = B_Q_HEADS // B_KV_HEADS
B_WIDTH = B_Q_HEADS * HEAD_DIM
B_KV_WIDTH = B_KV_HEADS * HEAD_DIM
C_HEADS = 4
C_HEAD_DIM = 256
C_WIDTH = C_HEADS * C_HEAD_DIM
C_CONV = 4
C_CHUNK = 128
N_BUCKETS = 32
MAX_DISTANCE = 2048
B_BIAS_OFF = A_GROUPS * A_HEADS
LOG_BUCKET_RATIO = math.log(MAX_DISTANCE / (N_BUCKETS // 2))
BAND = 128
N_EXPERTS = 32
TOP_K = 4
D_FF = 1024
SWIGLU_LIMIT = 7.0
SWIGLU_ALPHA = 1.702
DN_ALPHA = (2 * DEPTH) ** 0.25
LN_EPS = 1e-5
NEG = -1e30

OFF_BQ = 0
OFF_CQ = 1024
OFF_CK = 2048
OFF_CV = 3072
OFF_CO = 4096
OFF_BK = 5120
OFF_BV = 5248
C_PROJ = 5376
A_QKV_G = 3 * A_WIDTH
C_PROJ_A = A_GROUPS * A_QKV_G
SRC_AQ, SRC_AK, SRC_AV = 0, 768, 1536
SRC_BQ, SRC_BK, SRC_BV = 2304, 3328, 3456
SRC_CQ, SRC_CK, SRC_CV = 3584, 4608, 5632
SRC_CI, SRC_CF, SRC_CO = 6656, 6660, 6664

LANE = 128
MOE_ROWS = 256
TOK_CHUNK = 4096
VMEM_LIMIT = 48 * 1024 * 1024


def _cparams(sem):
    return pltpu.CompilerParams(dimension_semantics=sem, vmem_limit_bytes=VMEM_LIMIT)


def _sigmoid(x):
    return 1.0 / (1.0 + jnp.exp(-x))


def _log_sigmoid(x):
    return jnp.minimum(x, 0.0) - jnp.log(1.0 + jnp.exp(-jnp.abs(x)))


def _silu(x):
    return x * _sigmoid(x)


def _dot(a, b):
    return jnp.dot(a, b, preferred_element_type=F32)


def _dot_nt(a, b):
    return lax.dot_general(a, b, (((1,), (1,)), ((), ())), preferred_element_type=F32)


def _dot_tn(a, b):
    return lax.dot_general(a, b, (((0,), (0,)), ((), ())), preferred_element_type=F32)


def _ada_kernel(c_ref, w_ref, b_ref, o_ref):
    a = _silu(c_ref[...]).astype(BF16)
    o_ref[0] = _dot(a, w_ref[0].astype(BF16)) + b_ref[0]


def ada_all(c, ada_w, ada_b):
    n, d = c.shape
    tn = 1536
    nt = ada_w.shape[2] // tn
    n_layers = ada_w.shape[0]
    return pl.pallas_call(
        _ada_kernel,
        grid=(n_layers, nt),
        in_specs=[pl.BlockSpec((n, d), lambda l, j: (0, 0)),
                  pl.BlockSpec((1, d, tn), lambda l, j: (l, 0, j)),
                  pl.BlockSpec((1, 1, tn), lambda l, j: (l, 0, j))],
        out_specs=pl.BlockSpec((1, n, tn), lambda l, j: (l, 0, j)),
        out_shape=jax.ShapeDtypeStruct((n_layers, n, ada_w.shape[2]), F32),
        compiler_params=_cparams(("parallel", "arbitrary")),
        name="ada",
    )(c, ada_w, ada_b.reshape(n_layers, 1, -1))


def _ada_spec(ada, comp, tm, rows_per_cond):
    if rows_per_cond > 1:
        tiles = rows_per_cond // tm
        return pl.BlockSpec((1, 1, D_MODEL), lambda i, *_: ((i // tiles) * 6 + comp, 0, 0))
    return pl.BlockSpec((tm, D_MODEL), lambda i, *_: (i, comp))


def _ada_val(ref, rows_per_cond):
    return ref[0] if rows_per_cond > 1 else ref[...]


def _ada_arg(ada_l, rows_per_cond):
    if rows_per_cond > 1:
        return ada_l.reshape(-1, 1, D_MODEL)
    return ada_l


def _inproj_kernel(x_ref, sh_ref, sc_ref, w_ref, wa_ref, wif_ref, bif_ref,
                   o_ref, a0_ref, a1_ref, a2_ref, gif_ref, u_scr, y_scr, *, rpc, dils):
    @pl.when(pl.program_id(1) == 0)
    def _():
        u = x_ref[...] * (1.0 + _ada_val(sc_ref, rpc)) + _ada_val(sh_ref, rpc)
        ub = u.astype(BF16)
        u_scr[...] = ub
        gif_ref[...] = _dot(ub, wif_ref[...]) + bif_ref[...]
        tm = ub.shape[0]
        for g, (d, a_ref) in enumerate(zip(dils, (a0_ref, a1_ref, a2_ref))):
            y = _dot(ub, wa_ref[:, g * A_QKV_G:(g + 1) * A_QKV_G])
            if d == 1:
                a_ref[0, 0] = y.astype(a_ref.dtype)
            else:
                n_lt = A_QKV_G // LANE
                for c in range(n_lt):
                    y_scr[c] = y[:, c * LANE:(c + 1) * LANE]
                for r in range(d):
                    rows = [y_scr[c, pl.ds(r, tm // d, stride=d), :] for c in range(n_lt)]
                    a_ref[0, r] = jnp.concatenate(rows, axis=1).astype(a_ref.dtype)

    o_ref[...] = _dot(u_scr[...], w_ref[...]).astype(o_ref.dtype)


def inproj(x, ada_l, w_main, w_a, w_if, b_if, rows_per_cond, dils, out_dtype):
    m = x.shape[0]
    tm = min(m, 512)
    tn = 768
    seq = rows_per_cond if rows_per_cond > 1 else m
    tiles = seq // tm
    a = _ada_arg(ada_l, rows_per_cond)

    def a_spec(d):
        return pl.BlockSpec((1, d, tm // d, A_QKV_G), lambda i, j: (i // tiles, 0, i % tiles, 0))

    return pl.pallas_call(
        functools.partial(_inproj_kernel, rpc=rows_per_cond, dils=dils),
        grid=(m // tm, C_PROJ // tn),
        in_specs=[pl.BlockSpec((tm, D_MODEL), lambda i, j: (i, 0)),
                  _ada_spec(a, 0, tm, rows_per_cond),
                  _ada_spec(a, 1, tm, rows_per_cond),
                  pl.BlockSpec((D_MODEL, tn), lambda i, j: (0, j)),
                  pl.BlockSpec((D_MODEL, C_PROJ_A), lambda i, j: (0, 0)),
                  pl.BlockSpec((D_MODEL, LANE), lambda i, j: (0, 0)),
                  pl.BlockSpec((1, LANE), lambda i, j: (0, 0))],
        out_specs=[pl.BlockSpec((tm, tn), lambda i, j: (i, j))] + [a_spec(d) for d in dils]
                  + [pl.BlockSpec((tm, LANE), lambda i, j: (i, 0))],
        out_shape=[jax.ShapeDtypeStruct((m, C_PROJ), out_dtype)]
                  + [jax.ShapeDtypeStruct((m // seq, d, seq // d, A_QKV_G), out_dtype) for d in dils]
                  + [jax.ShapeDtypeStruct((m, LANE), F32)],
        scratch_shapes=[pltpu.VMEM((tm, D_MODEL), BF16), pltpu.VMEM((A_QKV_G // LANE, tm, LANE), F32)],
        compiler_params=_cparams(("parallel", "arbitrary")),
        name="inproj",
    )(x, a, a, w_main, w_a, w_if, b_if)


def _rel_bucket(dist):
    exact = N_BUCKETS // 2
    d = jnp.maximum(dist, 1).astype(F32)
    far = exact + (jnp.log(d / exact) / LOG_BUCKET_RATIO * (N_BUCKETS - exact)).astype(jnp.int32)
    return jnp.where(dist < exact, dist, jnp.minimum(far, N_BUCKETS - 1))


def band_bias(table, dilation):
    t = jnp.arange(BAND)[:, None]
    s = jnp.arange(2 * BAND)[None, :]
    dist = BAND + t - s
    valid = (dist >= 0) & (dist <= BAND)
    b = table[_rel_bucket(jnp.maximum(dist, 0) * dilation)]
    return jnp.where(valid[None], b.transpose(2, 0, 1).astype(F32), NEG)


def cache_bias(table, dilation, width):
    dist = width - jnp.arange(width)
    b = table[_rel_bucket(dist)].T.astype(F32)
    return jnp.where((dist % dilation == 0)[None, :], b, NEG)


def _band_a_kernel(qc_ref, kc_ref, kp_ref, vc_ref, vp_ref, bias_ref, o_ref, l_ref, o_scr, l_scr, *, dil):
    has_prev = pl.program_id(1) > 0
    r = pl.program_id(2)
    q = qc_ref[0, 0]
    kk = jnp.concatenate([kp_ref[0, 0], kc_ref[0, 0]], axis=0)
    vv = jnp.concatenate([vp_ref[0, 0], vc_ref[0, 0]], axis=0)
    prev_col = lax.broadcasted_iota(jnp.int32, (BAND, 2 * BAND), 1) < BAND
    no_prev = jnp.where(prev_col, jnp.where(has_prev, 0.0, NEG), 0.0)
    outs, lses = [], []
    for h in range(A_HEADS):
        hs = slice(h * HEAD_DIM, (h + 1) * HEAD_DIM)
        s = _dot_nt(q[:, hs], kk[:, hs]) * (HEAD_DIM ** -0.5) + bias_ref[h] + no_prev
        m = jnp.max(s, axis=1, keepdims=True)
        p = jnp.exp(s - m)
        l = jnp.sum(p, axis=1, keepdims=True)
        outs.append(_dot(p.astype(BF16), vv[:, hs]) / l)
        lses.append(jnp.broadcast_to(m + jnp.log(l), (BAND, HEAD_DIM)))
    o_all = jnp.concatenate(outs, axis=1)
    l_all = jnp.concatenate(lses, axis=1)
    if dil == 1:
        o_ref[0] = o_all
        l_ref[0] = l_all
    else:
        n_lt = A_WIDTH // LANE
        for rr in range(dil):
            @pl.when(r == rr)
            def _(rr=rr):
                for c in range(n_lt):
                    o_scr[c, pl.ds(rr, BAND, stride=dil), :] = o_all[:, c * LANE:(c + 1) * LANE]
                    l_scr[c, pl.ds(rr, BAND, stride=dil), :] = l_all[:, c * LANE:(c + 1) * LANE]

        @pl.when(r == dil - 1)
        def _():
            o_ref[0] = jnp.concatenate([o_scr[c] for c in range(n_lt)], axis=1)
            l_ref[0] = jnp.concatenate([l_scr[c] for c in range(n_lt)], axis=1)


def band_attention_a(qkv, bias, dil):
    bsz, _, m, _ = qkv.shape
    seq = m * dil

    def cur(j):
        return pl.BlockSpec((1, 1, BAND, A_WIDTH), lambda b, n, r: (b, r, n, j))

    def prev(j):
        return pl.BlockSpec((1, 1, BAND, A_WIDTH), lambda b, n, r: (b, r, jnp.maximum(n - 1, 0), j))

    o_spec = pl.BlockSpec((1, BAND * dil, A_WIDTH), lambda b, n, r: (b, n, 0))
    outs = pl.pallas_call(
        functools.partial(_band_a_kernel, dil=dil),
        grid=(bsz, m // BAND, dil),
        in_specs=[cur(0), cur(1), prev(1), cur(2), prev(2),
                  pl.BlockSpec((A_HEADS, BAND, 2 * BAND), lambda b, n, r: (0, 0, 0))],
        out_specs=[o_spec, o_spec],
        out_shape=[jax.ShapeDtypeStruct((bsz, seq, A_WIDTH), F32)] * 2,
        scratch_shapes=[pltpu.VMEM((A_WIDTH // LANE, BAND * dil, LANE), F32)] * 2,
        compiler_params=_cparams(("parallel", "arbitrary", "arbitrary")),
        name="band_attn_a_d%d" % dil,
    )(qkv, qkv, qkv, qkv, qkv, bias)
    return [o.reshape(bsz * seq, A_WIDTH) for o in outs]


def _band_b_kernel(q_ref, kc_ref, kp_ref, vc_ref, vp_ref, bias_ref, sink_ref, o_ref):
    has_prev = pl.program_id(1) > 0
    q = q_ref[0]
    kk = jnp.concatenate([kp_ref[0], kc_ref[0]], axis=0)
    vv = jnp.concatenate([vp_ref[0], vc_ref[0]], axis=0)
    prev_col = lax.broadcasted_iota(jnp.int32, (B_REP * BAND, 2 * BAND), 1) < BAND
    no_prev = jnp.where(prev_col, jnp.where(has_prev, 0.0, NEG), 0.0)
    for kv in range(B_KV_HEADS):
        ks = slice(kv * HEAD_DIM, (kv + 1) * HEAD_DIM)
        qs = jnp.concatenate([q[:, (kv * B_REP + j) * HEAD_DIM:(kv * B_REP + j + 1) * HEAD_DIM]
                              for j in range(B_REP)], axis=0)
        s = _dot_nt(qs, kk[:, ks]) * (HEAD_DIM ** -0.5) + bias_ref[kv] + no_prev
        m = jnp.max(s, axis=1, keepdims=True)
        p = jnp.exp(s - m)
        den = jnp.sum(p, axis=1, keepdims=True) + jnp.exp(sink_ref[kv] - m)
        o = _dot(p.astype(BF16), vv[:, ks]) / den
        for j in range(B_REP):
            h = kv * B_REP + j
            o_ref[0, :, h * HEAD_DIM:(h + 1) * HEAD_DIM] = o[j * BAND:(j + 1) * BAND].astype(o_ref.dtype)


def band_attention_b(proj, bsz, seq, bias, sink):
    p3 = proj.reshape(bsz, seq, C_PROJ)
    kb, vb = OFF_BK // B_KV_WIDTH, OFF_BV // B_KV_WIDTH

    def kv_spec(j, back):
        return pl.BlockSpec((1, BAND, B_KV_WIDTH), lambda b, n: (b, jnp.maximum(n - back, 0), j))

    out = pl.pallas_call(
        _band_b_kernel,
        grid=(bsz, seq // BAND),
        in_specs=[pl.BlockSpec((1, BAND, B_WIDTH), lambda b, n: (b, n, OFF_BQ // B_WIDTH)),
                  kv_spec(kb, 0), kv_spec(kb, 1), kv_spec(vb, 0), kv_spec(vb, 1),
                  pl.BlockSpec((B_KV_HEADS, B_REP * BAND, 2 * BAND), lambda b, n: (0, 0, 0)),
                  pl.BlockSpec((B_KV_HEADS, B_REP * BAND, 1), lambda b, n: (0, 0, 0))],
        out_specs=pl.BlockSpec((1, BAND, B_WIDTH), lambda b, n: (b, n, 0)),
        out_shape=jax.ShapeDtypeStruct((bsz, seq, B_WIDTH), BF16),
        compiler_params=_cparams(("parallel", "arbitrary")),
        name="band_attn_b",
    )(p3, p3, p3, p3, p3, bias, sink)
    return out.reshape(bsz * seq, B_WIDTH)


def _to_cols(row):
    return jnp.transpose(jnp.broadcast_to(row, (LANE, row.shape[1])))


def _cache_head(q_c, kn_c, vn_c, kt_ref, vt_ref, kto_ref, vto_ref, bias_row, b0, sink, width):
    nt = width // LANE
    lane = lax.broadcasted_iota(jnp.int32, (HEAD_DIM, LANE), 1)
    keep = lane < LANE - 1

    def shifted(src_ref, dst_ref, new_c):
        tiles = [src_ref[:, t * LANE:(t + 1) * LANE] for t in range(nt)]
        rot = [pltpu.roll(x, LANE - 1, 1) for x in tiles]
        if dst_ref is not None:
            for t in range(nt):
                nxt = rot[t + 1] if t + 1 < nt else new_c
                dst_ref[:, t * LANE:(t + 1) * LANE] = jnp.where(keep, rot[t], nxt)
        return tiles

    k_tiles = shifted(kt_ref, kto_ref, kn_c)
    s_tiles = [jnp.sum(k_tiles[t] * q_c, axis=0, keepdims=True) + bias_row[:, t * LANE:(t + 1) * LANE]
               for t in range(nt)]
    s0 = jnp.sum(q_c * kn_c, axis=0, keepdims=True)[:, 0:1] + b0
    m = s0
    for s in s_tiles:
        m = jnp.maximum(m, jnp.max(s, axis=1, keepdims=True))
    p_tiles = [jnp.exp(s - m) for s in s_tiles]
    p0 = jnp.exp(s0 - m)
    l = p0
    for p in p_tiles:
        l = l + jnp.sum(p, axis=1, keepdims=True)
    v_tiles = shifted(vt_ref, vto_ref, vn_c)
    acc = v_tiles[0] * p_tiles[0]
    for t in range(1, nt):
        acc = acc + v_tiles[t] * p_tiles[t]
    o = jnp.sum(acc, axis=1, keepdims=True) + p0 * vn_c[:, 0:1]
    den = l if sink is None else l + jnp.exp(sink - m)
    return o / den, m + jnp.log(l)


def _cache_step_kernel(*refs, n_alias):
    (pa_ref, pbq_ref, pbkv_ref, c1_ref, c2_ref, c3_ref, cb_ref,
     b1_ref, b2_ref, b3_ref, bb_ref, b0_ref, sink_ref) = refs[:13]
    oa_ref, la_ref, ob_ref, o1_ref, o2_ref, o3_ref, obc_ref = refs[13 + n_alias:]
    pa = _to_cols(pa_ref[0])
    scale = HEAD_DIM ** -0.5
    out_cols, lse_parts = [], []
    for g, (c_ref, co_ref, bias_ref) in enumerate(((c1_ref, o1_ref, b1_ref), (c2_ref, o2_ref, b2_ref),
                                                   (c3_ref, o3_ref, b3_ref))):
        width = c_ref.shape[-1]
        for h in range(A_HEADS):
            base = g * A_QKV_G + h * HEAD_DIM
            o, lse = _cache_head(pa[base:base + HEAD_DIM] * scale,
                                 pa[base + A_WIDTH:base + A_WIDTH + HEAD_DIM],
                                 pa[base + 2 * A_WIDTH:base + 2 * A_WIDTH + HEAD_DIM],
                                 c_ref.at[0, 0, 0, h], c_ref.at[0, 0, 1, h],
                                 co_ref.at[0, 0, 0, h], co_ref.at[0, 0, 1, h],
                                 bias_ref[h:h + 1, :], b0_ref[g * A_HEADS + h:g * A_HEADS + h + 1, 0:1],
                                 None, width)
            out_cols.append(jnp.broadcast_to(o, (HEAD_DIM, LANE)))
            lse_parts.append(jnp.broadcast_to(lse, (1, HEAD_DIM)))
    oa_ref[0] = jnp.transpose(jnp.concatenate(out_cols, axis=0))[0:1, :]
    la_ref[0] = jnp.concatenate(lse_parts, axis=1)
    qb = _to_cols(pbq_ref[0])
    kvb = _to_cols(pbkv_ref[0])
    out_cols = []
    for h in range(B_Q_HEADS):
        kv = h // B_REP
        first = h % B_REP == 0
        o, _ = _cache_head(qb[h * HEAD_DIM:(h + 1) * HEAD_DIM] * scale,
                           kvb[kv * HEAD_DIM:(kv + 1) * HEAD_DIM],
                           kvb[B_KV_WIDTH + kv * HEAD_DIM:B_KV_WIDTH + (kv + 1) * HEAD_DIM],
                           cb_ref.at[0, 0, 0, kv], cb_ref.at[0, 0, 1, kv],
                           obc_ref.at[0, 0, 0, kv] if first else None,
                           obc_ref.at[0, 0, 1, kv] if first else None,
                           bb_ref[h:h + 1, :],
                           b0_ref[B_BIAS_OFF + h:B_BIAS_OFF + h + 1, 0:1],
                           sink_ref[h:h + 1, 0:1], B_WINDOW)
        out_cols.append(jnp.broadcast_to(o, (HEAD_DIM, LANE)))
    ob_ref[0] = jnp.transpose(jnp.concatenate(out_cols, axis=0))[0:1, :]


def cache_step(layer, proj, proj_a, caches_t, prev_outs, biases, b0, sink):
    n = proj.shape[0]
    p3 = proj.reshape(n, 1, C_PROJ)
    pa3 = proj_a.reshape(n, 1, C_PROJ_A)

    def cache_spec(c):
        return pl.BlockSpec((1, 1) + c.shape[2:], lambda i: (layer, i, 0, 0, 0, 0))

    def full(a):
        return pl.BlockSpec(a.shape, lambda i: (0,) * a.ndim)

    in_specs = ([pl.BlockSpec((1, 1, C_PROJ_A), lambda i: (i, 0, 0)),
                 pl.BlockSpec((1, 1, B_WIDTH), lambda i: (i, 0, OFF_BQ // B_WIDTH)),
                 pl.BlockSpec((1, 1, 2 * B_KV_WIDTH), lambda i: (i, 0, OFF_BK // (2 * B_KV_WIDTH)))]
                + [cache_spec(c) for c in caches_t] + [full(b) for b in biases] + [full(b0), full(sink)])
    args = [pa3, p3, p3, *caches_t, *biases, b0, sink]
    aliases = {}
    if prev_outs is not None:
        for j, po in enumerate(prev_outs):
            in_specs.append(pl.BlockSpec(memory_space=pl.ANY))
            aliases[len(args)] = 3 + j
            args.append(po)
    row = lambda w: pl.BlockSpec((1, 1, w), lambda i: (i, 0, 0))
    outs = pl.pallas_call(
        functools.partial(_cache_step_kernel, n_alias=len(aliases)),
        grid=(n,),
        in_specs=in_specs,
        out_specs=[row(A_QKV), row(A_QKV), row(B_WIDTH)] + [cache_spec(c) for c in caches_t],
        out_shape=[jax.ShapeDtypeStruct((n, 1, A_QKV), F32), jax.ShapeDtypeStruct((n, 1, A_QKV), F32),
                   jax.ShapeDtypeStruct((n, 1, B_WIDTH), F32)]
                  + [jax.ShapeDtypeStruct(c.shape, c.dtype) for c in caches_t],
        input_output_aliases=aliases,
        compiler_params=_cparams(("arbitrary",)),
        name="cache_step",
    )(*args)
    o_a, l_a, o_b = outs[0].reshape(n, A_QKV), outs[1].reshape(n, A_QKV), outs[2].reshape(n, B_WIDTH)
    return o_a, l_a, o_b, list(outs[3:])


def _shift_rows(x, tail, k):
    sh = pltpu.roll(x, k, 0)
    first = jnp.where(lax.broadcasted_iota(jnp.int32, (8, x.shape[1]), 0) < k,
                      pltpu.roll(tail, k, 0), sh[0:8])
    return jnp.concatenate([first, sh[8:]], axis=0)


def _mlstm_chunk_kernel(q_ref, k_ref, v_ref, o_ref, g_ref, gt_ref, cw_ref, cb_ref,
                        h_ref, mem_ref, nrm_ref, max_ref,
                        mem_s, nrm_s, m_s, tail_s):
    c = pl.program_id(1)
    L = C_CHUNK

    @pl.when(c == 0)
    def _():
        mem_s[...] = jnp.zeros_like(mem_s)
        nrm_s[...] = jnp.zeros_like(nrm_s)
        m_s[...] = jnp.zeros_like(m_s)
        tail_s[...] = jnp.zeros_like(tail_s)

    def conv(x_ref_, which):
        x = x_ref_[0].astype(F32)
        lo = which * C_WIDTH
        tail = tail_s[which]
        y = cb_ref[:, lo:lo + C_WIDTH] + x * cw_ref[3:4, lo:lo + C_WIDTH]
        for k in (1, 2, 3):
            y = y + _shift_rows(x, tail, k) * cw_ref[3 - k:4 - k, lo:lo + C_WIDTH]
        tail_s[which] = x[L - 8:L]
        return _silu(y)

    qa = conv(q_ref, 0)
    ka = conv(k_ref, 1) * (C_HEAD_DIM ** -0.5)
    va = v_ref[0]
    g = g_ref[0]
    gt = gt_ref[0]
    t_i = lax.broadcasted_iota(jnp.int32, (L, L), 0)
    s_i = lax.broadcasted_iota(jnp.int32, (L, L), 1)
    tri = s_i <= t_i
    for h in range(C_HEADS):
        hs = slice(h * C_HEAD_DIM, (h + 1) * C_HEAD_DIM)
        q, k, v = qa[:, hs], ka[:, hs], va[:, hs]
        lf_col = _log_sigmoid(g[:, 4 + h:5 + h])
        lf_row = _log_sigmoid(gt[4 + h:5 + h, :])
        ig_col = g[:, h:h + 1]
        ig_row = gt[h:h + 1, :]
        a_col = jnp.sum(jnp.where(tri, lf_row, 0.0), axis=1, keepdims=True)
        a_row = jnp.sum(jnp.where(t_i <= s_i, lf_col, 0.0), axis=0, keepdims=True)
        a_last = jnp.sum(lf_row, axis=1, keepdims=True)
        m_prev = m_s[h:h + 1, 0:1]
        inter = a_col + m_prev
        dmat = jnp.where(tri, a_col - a_row + ig_row, NEG)
        mt = jnp.maximum(inter, jnp.max(dmat, axis=1, keepdims=True))
        w_inter = jnp.exp(inter - mt)
        qb, kb, vb = q.astype(BF16), k.astype(BF16), v.astype(BF16)
        s = _dot_nt(qb, kb) * jnp.exp(dmat - mt)
        mem = mem_s[h]
        nrm = nrm_s[h:h + 1, :]
        num = w_inter * _dot(qb, mem.astype(BF16)) + _dot(s.astype(BF16), vb)
        den = w_inter * jnp.sum(q * nrm, axis=1, keepdims=True) + jnp.sum(s, axis=1, keepdims=True)
        hid = num / jnp.maximum(jnp.abs(den), jnp.exp(-mt))
        og = _sigmoid(o_ref[0, :, hs].astype(F32))
        h_ref[0, :, hs] = (hid * og).astype(h_ref.dtype)
        m_new = mt[L - 1:L, :]
        decay = jnp.exp(a_last + m_prev - m_new)
        w_s = jnp.exp(a_last - a_col + ig_col - m_new)
        kw_ = k * w_s
        mem_s[h] = decay * mem + _dot_tn(kw_.astype(BF16), vb)
        nrm_s[h:h + 1, :] = decay * nrm + jnp.sum(kw_, axis=0, keepdims=True)
        m_s[h:h + 1, :] = jnp.broadcast_to(m_new, (1, LANE))

    @pl.when(c == pl.num_programs(1) - 1)
    def _():
        mem_ref[0] = mem_s[...]
        nrm_ref[0] = nrm_s[0:C_HEADS, :]
        max_ref[0] = m_s[...]


def mlstm_prompt(proj, gif, bsz, seq, conv_w, conv_b):
    nc = seq // C_CHUNK
    p3 = proj.reshape(bsz, seq, C_PROJ)
    g3 = gif.reshape(bsz, seq, LANE)
    gt3 = jnp.swapaxes(g3[:, :, :8], 1, 2)

    def col(j):
        return pl.BlockSpec((1, C_CHUNK, C_WIDTH), lambda b, c: (b, c, j))

    outs = pl.pallas_call(
        _mlstm_chunk_kernel,
        grid=(bsz, nc),
        in_specs=[col(OFF_CQ // C_WIDTH), col(OFF_CK // C_WIDTH), col(OFF_CV // C_WIDTH),
                  col(OFF_CO // C_WIDTH),
                  pl.BlockSpec((1, C_CHUNK, LANE), lambda b, c: (b, c, 0)),
                  pl.BlockSpec((1, 8, C_CHUNK), lambda b, c: (b, 0, c)),
                  pl.BlockSpec((C_CONV, 2 * C_WIDTH), lambda b, c: (0, 0)),
                  pl.BlockSpec((1, 2 * C_WIDTH), lambda b, c: (0, 0))],
        out_specs=[pl.BlockSpec((1, C_CHUNK, C_WIDTH), lambda b, c: (b, c, 0)),
                   pl.BlockSpec((1, C_HEADS, C_HEAD_DIM, C_HEAD_DIM), lambda b, c: (b, 0, 0, 0)),
                   pl.BlockSpec((1, C_HEADS, C_HEAD_DIM), lambda b, c: (b, 0, 0)),
                   pl.BlockSpec((1, 8, LANE), lambda b, c: (b, 0, 0))],
        out_shape=[jax.ShapeDtypeStruct((bsz, seq, C_WIDTH), BF16),
                   jax.ShapeDtypeStruct((bsz, C_HEADS, C_HEAD_DIM, C_HEAD_DIM), F32),
                   jax.ShapeDtypeStruct((bsz, C_HEADS, C_HEAD_DIM), F32),
                   jax.ShapeDtypeStruct((bsz, 8, LANE), F32)],
        scratch_shapes=[pltpu.VMEM((C_HEADS, C_HEAD_DIM, C_HEAD_DIM), F32),
                        pltpu.VMEM((8, C_HEAD_DIM), F32),
                        pltpu.VMEM((8, LANE), F32),
                        pltpu.VMEM((2, 8, C_WIDTH), F32)],
        compiler_params=_cparams(("parallel", "arbitrary")),
        name="mlstm_chunk",
    )(p3, p3, p3, p3, g3, gt3, conv_w, conv_b.reshape(1, -1))
    oc, mem, nrm, mx = outs
    return oc.reshape(bsz * seq, C_WIDTH), mem, nrm, mx[:, :C_HEADS, 0]


def _mlstm_step_kernel(*refs, bt, n_alias):
    q_ref, k_ref, v_ref, o_ref, g_ref, cw_ref, cb_ref, conv_ref, mem4_ref, nrm_ref, max_ref = refs[:11]
    h_ref, memo4_ref, nrmo_ref, maxo_ref, convo_ref = refs[11 + n_alias:]
    mem_ref, memo_ref = mem4_ref.at[0], memo4_ref.at[0]
    for b in range(bt):
        cv0 = conv_ref[b]
        xq = q_ref[b:b + 1, :].astype(F32)
        xk = k_ref[b:b + 1, :].astype(F32)
        x = jnp.concatenate([xq, xk], axis=1)
        y = (cb_ref[...] + cv0[0:1] * cw_ref[0:1] + cv0[1:2] * cw_ref[1:2] + cv0[2:3] * cw_ref[2:3]
             + x * cw_ref[3:4])
        y = _silu(y)
        convo_ref[b] = jnp.concatenate([cv0[1:3], x], axis=0)
        g = g_ref[b:b + 1, :]
        vrow = v_ref[b:b + 1, :]
        ogate = _sigmoid(o_ref[b:b + 1, :].astype(F32))
        mrow = max_ref[0, b:b + 1, :]
        hs_out, m_out = [], []
        for h in range(C_HEADS):
            hs = slice(h * C_HEAD_DIM, (h + 1) * C_HEAD_DIM)
            q = y[:, hs]
            k = y[:, C_WIDTH + h * C_HEAD_DIM:C_WIDTH + (h + 1) * C_HEAD_DIM] * (C_HEAD_DIM ** -0.5)
            v = vrow[:, hs]
            lf = _log_sigmoid(g[:, 4 + h:5 + h])
            ig = g[:, h:h + 1]
            m_prev = mrow[:, h:h + 1]
            inter = lf + m_prev
            mt = jnp.maximum(inter, ig)
            w_inter = jnp.exp(inter - mt)
            qb = jnp.broadcast_to(q, (16, C_HEAD_DIM)).astype(BF16)
            kb = k.astype(BF16)
            s = jnp.sum(qb[0:1].astype(F32) * kb.astype(F32), axis=1, keepdims=True) * jnp.exp(ig - mt)
            mem = mem_ref[b, h]
            nrm = nrm_ref[b, h:h + 1, :]
            qm = _dot(qb, mem.astype(BF16))[0:1]
            num = w_inter * qm + s * v.astype(F32)
            den = w_inter * jnp.sum(q * nrm, axis=1, keepdims=True) + s
            hid = num / jnp.maximum(jnp.abs(den), jnp.exp(-mt))
            hs_out.append(hid)
            decay = jnp.exp(inter - mt)
            w_s = jnp.exp(ig - mt)
            kw_ = k * w_s
            row0 = lax.broadcasted_iota(jnp.int32, (16, C_HEAD_DIM), 0) == 0
            k16 = jnp.where(row0, jnp.broadcast_to(kw_, (16, C_HEAD_DIM)), 0.0).astype(BF16)
            v16 = jnp.broadcast_to(v, (16, C_HEAD_DIM)).astype(BF16)
            memo_ref[b, h] = decay * mem + _dot_tn(k16, v16)
            nrmo_ref[b, h:h + 1, :] = decay * nrm + kw_
            m_out.append(mt)
        h_ref[b:b + 1, :] = (jnp.concatenate(hs_out, axis=1) * ogate).astype(h_ref.dtype)
        maxo_ref[0, b:b + 1, :] = jnp.concatenate(m_out, axis=1)


def mlstm_step(layer, proj, gif, conv_w, conv_b, conv0, mem_all, mem_prev, nrm0, max0):
    n = proj.shape[0]
    bt = 8

    def col(j):
        return pl.BlockSpec((bt, C_WIDTH), lambda i: (i, j))

    mem_spec = pl.BlockSpec((1, bt, C_HEADS, C_HEAD_DIM, C_HEAD_DIM), lambda i: (layer, i, 0, 0, 0))
    state_specs = [pl.BlockSpec((bt, C_CONV - 1, 2 * C_WIDTH), lambda i: (i, 0, 0)),
                   mem_spec,
                   pl.BlockSpec((bt, C_HEADS, C_HEAD_DIM), lambda i: (i, 0, 0)),
                   pl.BlockSpec((1, bt, C_HEADS), lambda i: (i, 0, 0))]
    in_specs = [col(OFF_CQ // C_WIDTH), col(OFF_CK // C_WIDTH), col(OFF_CV // C_WIDTH),
                col(OFF_CO // C_WIDTH),
                pl.BlockSpec((bt, LANE), lambda i: (i, 0)),
                pl.BlockSpec((C_CONV, 2 * C_WIDTH), lambda i: (0, 0)),
                pl.BlockSpec((1, 2 * C_WIDTH), lambda i: (0, 0))] + state_specs
    args = [proj, proj, proj, proj, gif, conv_w, conv_b.reshape(1, -1), conv0, mem_all, nrm0,
            max0.reshape(n // bt, bt, C_HEADS)]
    aliases = {}
    if mem_prev is not None:
        in_specs.append(pl.BlockSpec(memory_space=pl.ANY))
        aliases[len(args)] = 1
        args.append(mem_prev)
    outs = pl.pallas_call(
        functools.partial(_mlstm_step_kernel, bt=bt, n_alias=len(aliases)),
        grid=(n // bt,),
        in_specs=in_specs,
        out_specs=[pl.BlockSpec((bt, C_WIDTH), lambda i: (i, 0)),
                   mem_spec, state_specs[2], state_specs[3], state_specs[0]],
        out_shape=[jax.ShapeDtypeStruct((n, C_WIDTH), F32),
                   jax.ShapeDtypeStruct(mem_all.shape, F32),
                   jax.ShapeDtypeStruct(nrm0.shape, F32),
                   jax.ShapeDtypeStruct((n // bt, bt, C_HEADS), F32),
                   jax.ShapeDtypeStruct(conv0.shape, F32)],
        input_output_aliases=aliases,
        compiler_params=_cparams(("arbitrary",)),
        name="mlstm_step",
    )(*args)
    oc, mem1, nrm1, max1, conv1 = outs
    return oc, mem1, nrm1, max1.reshape(n, C_HEADS), conv1


def _layer_norm(z, g, b):
    mu = jnp.mean(z, axis=1, keepdims=True)
    zc = z - mu
    var = jnp.mean(zc * zc, axis=1, keepdims=True)
    return zc * lax.rsqrt(var + LN_EPS) * g + b


def _merge_kernel(x_ref, sh1_ref, sc1_ref, g1_ref, sh2_ref, sc2_ref,
                  o1_ref, o2_ref, o3_ref, l1_ref, l2_ref, l3_ref, ob_ref, oc_ref,
                  gw_ref, gb_ref, wa_ref, wb_ref, wc_ref, wo_ref, lg_ref, lb_ref, rwh_ref, rwl_ref, rb_ref,
                  x1_ref, u2_ref, te_ref, tw_ref, *, rpc):
    D = D_MODEL
    x = x_ref[...]
    u = (x * (1.0 + _ada_val(sc1_ref, rpc)) + _ada_val(sh1_ref, rpc)).astype(BF16)
    l1, l2, l3 = l1_ref[...], l2_ref[...], l3_ref[...]
    lm = jnp.maximum(jnp.maximum(l1, l2), l3)
    e1, e2, e3 = jnp.exp(l1 - lm), jnp.exp(l2 - lm), jnp.exp(l3 - lm)
    oa = (e1 * o1_ref[...] + e2 * o2_ref[...] + e3 * o3_ref[...]) / (e1 + e2 + e3)
    merged = None
    for idx, (lhs, w_ref) in enumerate(((oa.astype(BF16), wa_ref), (ob_ref[...].astype(BF16), wb_ref),
                                          (oc_ref[...].astype(BF16), wc_ref))):
        gate = _sigmoid(_dot(u, gw_ref[:, idx * D:(idx + 1) * D]) + gb_ref[:, idx * D:(idx + 1) * D])
        term = gate * _dot(lhs, w_ref[...])
        merged = term if merged is None else merged + term
    y = _dot(merged.astype(BF16), wo_ref[...])
    x1 = _layer_norm(DN_ALPHA * x + _ada_val(g1_ref, rpc) * y, lg_ref[...], lb_ref[...])
    x1_ref[...] = x1
    u2 = x1 * (1.0 + _ada_val(sc2_ref, rpc)) + _ada_val(sh2_ref, rpc)
    u2_ref[...] = u2
    uh = u2.astype(BF16)
    ul = (u2 - uh.astype(F32)).astype(BF16)
    logits = _dot(uh, rwh_ref[...]) + _dot(uh, rwl_ref[...]) + _dot(ul, rwh_ref[...]) + rb_ref[...]
    lane = lax.broadcasted_iota(jnp.int32, logits.shape, 1)
    lanef = lane.astype(F32)
    logits = jnp.where(lane < N_EXPERTS, logits, -jnp.inf)
    te = jnp.zeros(logits.shape, F32)
    tw = jnp.zeros(logits.shape, F32)
    top0 = None
    for k in range(TOP_K):
        mx = jnp.max(logits, axis=1, keepdims=True)
        idx = jnp.min(jnp.where(logits == mx, lanef, float(LANE)), axis=1, keepdims=True)
        if top0 is None:
            top0 = mx
        te = jnp.where(lane == k, idx, te)
        tw = jnp.where(lane == k, jnp.exp(mx - top0), tw)
        logits = jnp.where(lanef == idx, -jnp.inf, logits)
    tw = tw / jnp.sum(tw, axis=1, keepdims=True)
    te_ref[...] = te.astype(jnp.int32)
    tw_ref[...] = tw


def merge(x, ada_l, o_a, l_a, ob, oc, gate_w, gate_b, wa, wb, wc, wo, ln_g, ln_b, rw_hi, rw_lo, rb,
          rows_per_cond):
    m = x.shape[0]
    tm = min(m, 256)
    D = D_MODEL
    a = _ada_arg(ada_l, rows_per_cond)

    def rows(w):
        return pl.BlockSpec((tm, w), lambda i: (i, 0))

    def full(shape):
        return pl.BlockSpec(shape, lambda i: (0,) * len(shape))

    in_specs = ([rows(D)] + [_ada_spec(a, c, tm, rows_per_cond) for c in (0, 1, 2, 3, 4)]
                + [rows(A_WIDTH)] * 6 + [rows(B_WIDTH), rows(C_WIDTH)]
                + [full((D, 3 * D)), full((1, 3 * D)), full((A_WIDTH, D)), full((B_WIDTH, D)),
                   full((C_WIDTH, D)), full((D, D)), full((1, D)), full((1, D)),
                   full((D, LANE)), full((D, LANE)), full((1, LANE))])
    return pl.pallas_call(
        functools.partial(_merge_kernel, rpc=rows_per_cond),
        grid=(m // tm,),
        in_specs=in_specs,
        out_specs=[rows(D), rows(D), rows(LANE), rows(LANE)],
        out_shape=[jax.ShapeDtypeStruct((m, D), F32), jax.ShapeDtypeStruct((m, D), F32),
                   jax.ShapeDtypeStruct((m, LANE), jnp.int32), jax.ShapeDtypeStruct((m, LANE), F32)],
        compiler_params=_cparams(("parallel",)),
        name="merge",
    )(x, a, a, a, a, a, *o_a, *l_a, ob, oc, gate_w, gate_b, wa, wb, wc, wo, ln_g, ln_b, rw_hi, rw_lo, rb)


def moe_plan(top_e, n_rows_block):
    t = top_e.shape[0]
    n_slots = t * TOP_K
    slot_e = top_e.reshape(-1)
    onehot = (slot_e[:, None] == jnp.arange(N_EXPERTS, dtype=jnp.int32)[None, :]).astype(jnp.int32)
    csum = jnp.cumsum(onehot, axis=0)
    counts = csum[-1]
    rank = jnp.sum((csum - onehot) * onehot, axis=1)
    padded = (counts + n_rows_block - 1) // n_rows_block * n_rows_block
    pend = jnp.cumsum(padded)
    pstart = pend - padded
    dest = (jnp.sum(onehot * pstart[None, :], axis=1) + rank).astype(jnp.int32)
    n_blocks = -(-n_slots // n_rows_block) + N_EXPERTS
    blk_start = jnp.arange(n_blocks, dtype=jnp.int32) * n_rows_block
    blk_e = jnp.minimum(jnp.sum((blk_start[:, None] >= pend[None, :]).astype(jnp.int32), axis=1),
                        N_EXPERTS - 1).astype(jnp.int32)
    n_used = (pend[-1:] // n_rows_block).astype(jnp.int32)
    return dest, blk_e, n_used, n_blocks


def _scatter_kernel(dest_ref, u_ref, xs_in_ref, xs_ref, sem, *, tt):
    del xs_in_ref
    base = pl.program_id(0) * tt * TOP_K

    def copy(r, k):
        d = dest_ref[base + r * TOP_K + k]
        return pltpu.make_async_copy(u_ref.at[pl.ds(r, 1)], xs_ref.at[pl.ds(d, 1)], sem)

    def issue(r, carry):
        for k in range(TOP_K):
            copy(r, k).start()
        return carry

    def drain(r, carry):
        for k in range(TOP_K):
            copy(r, k).wait()
        return carry

    lax.fori_loop(0, tt, issue, 0)
    lax.fori_loop(0, tt, drain, 0)


def moe_scatter(u2, dest, xs):
    t = u2.shape[0]
    tt = min(t, 128)
    for c0 in range(0, t, TOK_CHUNK):
        nt = min(TOK_CHUNK, t - c0)
        xs = pl.pallas_call(
            functools.partial(_scatter_kernel, tt=tt),
            grid_spec=pltpu.PrefetchScalarGridSpec(
                num_scalar_prefetch=1,
                grid=(nt // tt,),
                in_specs=[pl.BlockSpec((tt, D_MODEL), lambda i, d: (i, 0)),
                          pl.BlockSpec(memory_space=pl.ANY)],
                out_specs=pl.BlockSpec(memory_space=pl.ANY),
                scratch_shapes=[pltpu.SemaphoreType.DMA]),
            out_shape=jax.ShapeDtypeStruct(xs.shape, xs.dtype),
            input_output_aliases={2: 0},
            compiler_params=_cparams(("arbitrary",)),
            name="moe_scatter",
        )(lax.slice_in_dim(dest, c0 * TOP_K, (c0 + nt) * TOP_K), lax.slice_in_dim(u2, c0, c0 + nt), xs)
    return xs


def _expert_kernel(be_ref, nu_ref, x_ref, uw_ref, ub_ref, dw_ref, db_ref, o_ref):
    del be_ref
    i = pl.program_id(0)

    @pl.when(i < nu_ref[0])
    def _():
        hid = _dot(x_ref[...].astype(BF16), uw_ref[0]) + ub_ref[0]
        gate = jnp.minimum(hid[:, :D_FF], SWIGLU_LIMIT)
        up = jnp.clip(hid[:, D_FF:], -SWIGLU_LIMIT, SWIGLU_LIMIT)
        act = (up + 1.0) * gate * _sigmoid(SWIGLU_ALPHA * gate)
        o_ref[...] = _dot(act.astype(BF16), dw_ref[0]) + db_ref[0]

    @pl.when(i >= nu_ref[0])
    def _():
        o_ref[...] = jnp.zeros_like(o_ref)


def moe_experts(xs, blk_e, n_used, up_w, up_b, down_w, down_b):
    rows = xs.shape[0]
    mb = MOE_ROWS
    return pl.pallas_call(
        _expert_kernel,
        grid_spec=pltpu.PrefetchScalarGridSpec(
            num_scalar_prefetch=2,
            grid=(rows // mb,),
            in_specs=[pl.BlockSpec((mb, D_MODEL), lambda i, be, nu: (i, 0)),
                      pl.BlockSpec((1, D_MODEL, 2 * D_FF), lambda i, be, nu: (be[i], 0, 0)),
                      pl.BlockSpec((1, 1, 2 * D_FF), lambda i, be, nu: (be[i], 0, 0)),
                      pl.BlockSpec((1, D_FF, D_MODEL), lambda i, be, nu: (be[i], 0, 0)),
                      pl.BlockSpec((1, 1, D_MODEL), lambda i, be, nu: (be[i], 0, 0))],
            out_specs=pl.BlockSpec((mb, D_MODEL), lambda i, be, nu: (i, 0))),
        out_shape=jax.ShapeDtypeStruct((rows, D_MODEL), F32),
        compiler_params=_cparams(("arbitrary",)),
        name="moe_experts",
    )(blk_e, n_used, xs, up_w, up_b.reshape(N_EXPERTS, 1, -1), down_w, down_b.reshape(N_EXPERTS, 1, -1))


def _combine_kernel(dest_ref, rows_ref, tw_ref, x_ref, g2_ref, lg_ref, lb_ref, o_ref, buf, sem, *, tt, rpc):
    i = pl.program_id(0)
    n = pl.num_programs(0)

    def copy(step, slot, r, k):
        d = dest_ref[(step * tt + r) * TOP_K + k]
        return pltpu.make_async_copy(rows_ref.at[pl.ds(d, 1)], buf.at[slot, k, pl.ds(r, 1)], sem.at[slot])

    def issue(step, slot):
        def body(r, carry):
            for k in range(TOP_K):
                copy(step, slot, r, k).start()
            return carry
        lax.fori_loop(0, tt, body, 0)

    def drain(step, slot):
        def body(r, carry):
            for k in range(TOP_K):
                copy(step, slot, r, k).wait()
            return carry
        lax.fori_loop(0, tt, body, 0)

    slot = lax.rem(i, 2)

    @pl.when(i == 0)
    def _():
        issue(0, 0)

    @pl.when(i + 1 < n)
    def _():
        issue(i + 1, 1 - slot)

    drain(i, slot)
    tw = tw_ref[...]
    f = None
    for k in range(TOP_K):
        term = buf[slot, k] * tw[:, k:k + 1]
        f = term if f is None else f + term
    z = DN_ALPHA * x_ref[...] + _ada_val(g2_ref, rpc) * f
    o_ref[...] = _layer_norm(z, lg_ref[...], lb_ref[...])


def moe_combine(rows, dest, top_w, x1, ada_l, ln_g, ln_b, rows_per_cond):
    t = x1.shape[0]
    tt = min(t, 64)
    a = _ada_arg(ada_l, rows_per_cond)
    outs = []
    for c0 in range(0, t, TOK_CHUNK):
        nt = min(TOK_CHUNK, t - c0)
        tile0 = c0 // tt
        if rows_per_cond > 1:
            tiles = rows_per_cond // tt
            g2_spec = pl.BlockSpec((1, 1, D_MODEL), lambda i, d, tile0=tile0, tiles=tiles: (((i + tile0) // tiles) * 6 + 5, 0, 0))
        else:
            g2_spec = pl.BlockSpec((tt, D_MODEL), lambda i, d, tile0=tile0: (i + tile0, 5))
        outs.append(pl.pallas_call(
            functools.partial(_combine_kernel, tt=tt, rpc=rows_per_cond),
            grid_spec=pltpu.PrefetchScalarGridSpec(
                num_scalar_prefetch=1,
                grid=(nt // tt,),
                in_specs=[pl.BlockSpec(memory_space=pl.ANY),
                          pl.BlockSpec((tt, LANE), lambda i, d, tile0=tile0: (i + tile0, 0)),
                          pl.BlockSpec((tt, D_MODEL), lambda i, d, tile0=tile0: (i + tile0, 0)),
                          g2_spec,
                          pl.BlockSpec((1, D_MODEL), lambda i, d: (0, 0)),
                          pl.BlockSpec((1, D_MODEL), lambda i, d: (0, 0))],
                out_specs=pl.BlockSpec((tt, D_MODEL), lambda i, d: (i, 0)),
                scratch_shapes=[pltpu.VMEM((2, TOP_K, tt, D_MODEL), F32),
                                pltpu.SemaphoreType.DMA((2,))]),
            out_shape=jax.ShapeDtypeStruct((nt, D_MODEL), F32),
            compiler_params=_cparams(("arbitrary",)),
            name="moe_combine",
        )(lax.slice_in_dim(dest, c0 * TOP_K, (c0 + nt) * TOP_K), rows, top_w, x1, a, ln_g, ln_b))
    return outs[0] if len(outs) == 1 else jnp.concatenate(outs, axis=0)


def _prep_weights(l, p):
    w = p["in_w"][l]
    order = [(SRC_BQ, B_WIDTH), (SRC_CQ, C_WIDTH), (SRC_CK, C_WIDTH), (SRC_CV, C_WIDTH), (SRC_CO, C_WIDTH),
             (SRC_BK, B_KV_WIDTH), (SRC_BV, B_KV_WIDTH)]
    w_main = jnp.concatenate([w[:, s:s + n] for s, n in order], axis=1).astype(BF16)
    order_a = [(src + g * A_WIDTH, A_WIDTH) for g in range(A_GROUPS) for src in (SRC_AQ, SRC_AK, SRC_AV)]
    w_a = jnp.concatenate([w[:, s:s + n] for s, n in order_a], axis=1).astype(BF16)
    sink = p["sink_b"][l]
    w_if = jnp.pad(w[:, SRC_CI:SRC_CI + 2 * C_HEADS], ((0, 0), (0, LANE - 2 * C_HEADS))).astype(BF16)
    b_if = jnp.pad(jnp.concatenate([p["igate_b"][l], p["fgate_b"][l]]), (0, LANE - 2 * C_HEADS)).reshape(1, LANE)
    rw = jnp.pad(p["router_w"][l], ((0, 0), (0, LANE - N_EXPERTS)))
    rw_hi = rw.astype(BF16)
    rw_lo = (rw - rw_hi.astype(F32)).astype(BF16)
    rb = jnp.pad(p["router_b"][l], (0, LANE - N_EXPERTS)).reshape(1, LANE)
    return dict(
        w_main=w_main, w_a=w_a, w_if=w_if, b_if=b_if,
        gate_w=p["gate_w"][l].astype(BF16), gate_b=p["gate_b"][l].reshape(1, -1),
        wa=p["br_a_w"][l].astype(BF16), wb=p["br_b_w"][l].astype(BF16), wc=p["br_c_w"][l].astype(BF16),
        wo=p["out_w"][l].astype(BF16),
        ln1_g=p["ln1_g"][l].reshape(1, -1), ln1_b=p["ln1_b"][l].reshape(1, -1),
        ln2_g=p["ln2_g"][l].reshape(1, -1), ln2_b=p["ln2_b"][l].reshape(1, -1),
        rw_hi=rw_hi, rw_lo=rw_lo, rb=rb,
        up_w=p["up_w"][l].astype(BF16), up_b=p["up_b"][l],
        down_w=p["down_w"][l].astype(BF16), down_b=p["down_b"][l],
        conv_w=p["conv_w"][l], conv_b=p["conv_b"][l],
        sink=jnp.broadcast_to(sink[:, None], (B_Q_HEADS, LANE)),
        sink_rows=jnp.repeat(sink.reshape(B_KV_HEADS, B_REP), BAND, axis=1)[:, :, None],
    )


def _moe_layer(x1, u2, top_e, top_w, ada_l, wl, xs, rows_per_cond):
    dest, blk_e, n_used, n_blocks = moe_plan(top_e[:, :TOP_K], MOE_ROWS)
    if xs is None:
        xs = jnp.zeros((n_blocks * MOE_ROWS, D_MODEL), F32)
    xs = moe_scatter(u2, dest, xs)
    rows = moe_experts(xs, blk_e, n_used, wl["up_w"], wl["up_b"], wl["down_w"], wl["down_b"])
    x2 = moe_combine(rows, dest, top_w, x1, ada_l, wl["ln2_g"], wl["ln2_b"], rows_per_cond)
    return x2, xs


def kernel(x_prompt, x_sample, c_prompt, c_sample, cache_a1_kv, cache_a2_kv, cache_a3_kv, cache_b_kv,
           state_c_mem, state_c_norm, state_c_max, state_c_conv, rel_bias, ada_w, ada_b, in_w, conv_w, conv_b,
           igate_b, fgate_b, sink_b, br_a_w, br_b_w, br_c_w, gate_w, gate_b, out_w, ln1_g, ln1_b,
           router_w, router_b, up_w, up_b, down_w, down_b, ln2_g, ln2_b):
    p = dict(in_w=in_w, conv_w=conv_w, conv_b=conv_b, igate_b=igate_b, fgate_b=fgate_b, sink_b=sink_b,
             br_a_w=br_a_w, br_b_w=br_b_w, br_c_w=br_c_w, gate_w=gate_w, gate_b=gate_b, out_w=out_w,
             ln1_g=ln1_g, ln1_b=ln1_b, router_w=router_w, router_b=router_b, up_w=up_w, up_b=up_b,
             down_w=down_w, down_b=down_b, ln2_g=ln2_g, ln2_b=ln2_b)
    bsz, seq, d = x_prompt.shape
    nb = x_sample.shape[0]
    n_layers = in_w.shape[0]
    ada = ada_all(jnp.concatenate([c_prompt, c_sample], axis=0), ada_w, ada_b)
    a_tabs = [rel_bias[:, g * A_HEADS:(g + 1) * A_HEADS] for g in range(A_GROUPS)]
    b_tab = rel_bias[:, B_BIAS_OFF:B_BIAS_OFF + B_Q_HEADS]
    band_tabs = [band_bias(a_tabs[g], A_DILATIONS[g]) for g in range(A_GROUPS)]
    band_tab_b = band_bias(b_tab, 1).reshape(B_KV_HEADS, B_REP * BAND, 2 * BAND)
    cache_tabs = ([cache_bias(a_tabs[g], A_DILATIONS[g], A_WINDOWS[g]) for g in range(A_GROUPS)]
                  + [cache_bias(b_tab, 1, B_WINDOW)])
    self_bias = jnp.broadcast_to(rel_bias[_rel_bucket(jnp.zeros((), jnp.int32))][:, None], (rel_bias.shape[1], LANE))
    caches_t = [jnp.transpose(c, (0, 1, 3, 4, 5, 2)) for c in (cache_a1_kv, cache_a2_kv, cache_a3_kv, cache_b_kv)]

    xp = x_prompt.reshape(bsz * seq, d)
    xsm = x_sample.reshape(nb, d)
    xs_p = xs_s = None
    st_p = [[] for _ in range(8)]
    st_s = [[] for _ in range(3)]
    new_caches = mem_s = None
    for l in range(n_layers):
        wl = _prep_weights(l, p)
        ada_p = ada[l, :bsz]
        proj, qkv0, qkv1, qkv2, gif = inproj(xp, ada_p, wl["w_main"], wl["w_a"], wl["w_if"], wl["b_if"], seq,
                                             A_DILATIONS, BF16)
        qkvs = (qkv0, qkv1, qkv2)
        o_a, l_a = [], []
        for g in range(A_GROUPS):
            o, lse = band_attention_a(qkvs[g], band_tabs[g], A_DILATIONS[g])
            o_a.append(o)
            l_a.append(lse)
        ob = band_attention_b(proj, bsz, seq, band_tab_b, wl["sink_rows"])
        oc, mem1, nrm1, max1 = mlstm_prompt(proj, gif, bsz, seq, wl["conv_w"], wl["conv_b"])
        x1, u2, top_e, top_w = merge(xp, ada_p, o_a, l_a, ob, oc, wl["gate_w"], wl["gate_b"], wl["wa"], wl["wb"],
                                     wl["wc"], wl["wo"], wl["ln1_g"], wl["ln1_b"], wl["rw_hi"], wl["rw_lo"],
                                     wl["rb"], seq)
        xp, xs_p = _moe_layer(x1, u2, top_e, top_w, ada_p, wl, xs_p, seq)
        p3 = proj.reshape(bsz, seq, C_PROJ)
        for g in range(A_GROUPS):
            dil = A_DILATIONS[g]
            wnd = min(A_WINDOWS[g], seq)
            kv = qkvs[g][:, :, (seq - wnd) // dil:, A_WIDTH:]
            kv = jnp.swapaxes(kv, 1, 2).astype(F32)
            st_p[g].append(kv.reshape(bsz, wnd, 2, A_HEADS, HEAD_DIM))
        wnd = min(B_WINDOW, seq)
        kk = p3[:, seq - wnd:, OFF_BK:OFF_BK + B_KV_WIDTH]
        vv = p3[:, seq - wnd:, OFF_BV:OFF_BV + B_KV_WIDTH]
        st_p[3].append(jnp.stack([kk, vv], axis=2).astype(F32).reshape(bsz, wnd, 2, B_KV_HEADS, HEAD_DIM))
        st_p[4].append(mem1)
        st_p[5].append(nrm1)
        st_p[6].append(max1)
        st_p[7].append(p3[:, seq - (C_CONV - 1):, OFF_CQ:OFF_CQ + 2 * C_WIDTH].astype(F32))
        ada_s = ada[l, bsz:]
        proj_s, sq0, sq1, sq2, gif_s = inproj(xsm, ada_s, wl["w_main"], wl["w_a"], wl["w_if"], wl["b_if"], 1,
                                              (1, 1, 1), F32)
        proj_sa = jnp.concatenate([t.reshape(nb, A_QKV_G) for t in (sq0, sq1, sq2)], axis=1)
        oa_s, la_s, ob, new_caches = cache_step(l, proj_s, proj_sa, caches_t, new_caches, cache_tabs,
                                                self_bias, wl["sink"])
        o_a = [oa_s[:, g * A_WIDTH:(g + 1) * A_WIDTH] for g in range(A_GROUPS)]
        l_a = [la_s[:, g * A_WIDTH:(g + 1) * A_WIDTH] for g in range(A_GROUPS)]
        oc, mem_s, nrm1, max1, conv1 = mlstm_step(l, proj_s, gif_s, wl["conv_w"], wl["conv_b"], state_c_conv[l],
                                                  state_c_mem, mem_s, state_c_norm[l], state_c_max[l])
        x1, u2, top_e, top_w = merge(xsm, ada_s, o_a, l_a, ob, oc, wl["gate_w"], wl["gate_b"], wl["wa"], wl["wb"],
                                     wl["wc"], wl["wo"], wl["ln1_g"], wl["ln1_b"], wl["rw_hi"], wl["rw_lo"],
                                     wl["rb"], 1)
        xsm, xs_s = _moe_layer(x1, u2, top_e, top_w, ada_s, wl, xs_s, 1)
        st_s[0].append(nrm1)
        st_s[1].append(max1)
        st_s[2].append(conv1)

    a1s, a2s, a3s, bs = [jnp.transpose(c, (0, 1, 5, 2, 3, 4)) for c in new_caches]
    sp = [jnp.stack(t) for t in st_p]
    ss = [jnp.stack(t) for t in st_s]
    return (xp.reshape(bsz, seq, d), xsm.reshape(nb, 1, d),
            sp[0], a1s, sp[1], a2s, sp[2], a3s, sp[3], bs,
            sp[4], mem_s, sp[5], ss[0], sp[6], ss[1], sp[7], ss[2])
```
